```python
import math
import jax, jax.numpy as jnp
from jax import lax
import numpy as np

D_MODEL = 2048
BATCH = 4
SEQ = 2048
DEPTH = 1
DEC_BATCH = 8
DEC_SEQ = 16
PAST_LEN = 1024

CHUNK = 64
D_MIX = D_MODEL
D_RWKV = D_MIX // 2
HEAD_DIM = 64
N_RWKV_HEADS = D_RWKV // HEAD_DIM
D_CONV = D_MIX - D_RWKV
CONV_WIDTH = 31
LORA_DECAY = 64
LORA_ICLR = 64
LORA_GATE = 128
D_SHIFT = 3 * D_RWKV + LORA_DECAY + LORA_ICLR + LORA_GATE
D_IN = D_SHIFT + 2 * D_CONV
N_EXPERTS = 256
TOP_K = 8
N_EXPERT_GROUPS = 8
TOPK_GROUPS = 4
D_EXPERT = 576
D_SHARED = 576
ROUTED_SCALE = 2.5
ALPHA = (2.0 * DEPTH) ** 0.25
BETA = (8.0 * DEPTH) ** -0.25
LN_EPS = 1e-5
GN_EPS = 64e-5

kernel_name = 'hybrid_rwkv7_conformer_moe_stream_step'


def layer_norm(x, g, b, eps=LN_EPS):
    xf = x.astype(jnp.float32)
    mu = jnp.mean(xf, axis=-1, keepdims=True)
    var = jnp.mean(jnp.square(xf - mu), axis=-1, keepdims=True)
    y = (xf - mu) * lax.rsqrt(var + eps)
    return (y * g.astype(jnp.float32) + b.astype(jnp.float32)).astype(x.dtype)


def wkv_step(S, inp):
    r, w, k, v, kk, b = inp
    sa = jnp.einsum('bhvk,bhk->bhv', S, -kk)
    S = S * w[:, :, None, :] + sa[..., None] * b[:, :, None, :] + v[..., None] * k[:, :, None, :]
    o = jnp.einsum('bhvk,bhk->bhv', S, r)
    return S, o


def rwkv7_group(p_rwkv, shift_prev, S0, p):
    B, T, _ = p_rwkv.shape
    H, N = N_RWKV_HEADS, HEAD_DIM
    f32 = jnp.float32
    prev = jnp.concatenate([shift_prev.astype(p_rwkv.dtype), p_rwkv[:, :-1]], axis=1)
    xs = p_rwkv + (prev - p_rwkv) * p['mu_shift']
    o1, o2, o3 = D_RWKV, 2 * D_RWKV, 3 * D_RWKV
    o4 = o3 + LORA_DECAY
    o5 = o4 + LORA_ICLR
    r, k, v = xs[..., :o1], xs[..., o1:o2], xs[..., o2:o3]
    wd, ad, gd = xs[..., o3:o4], xs[..., o4:o5], xs[..., o5:]
    w_log = -jax.nn.softplus(-(p['w0'] + jnp.tanh(wd) @ p['w2']).astype(f32)) - 0.5
    decay = jnp.exp(-jnp.exp(w_log))
    a = jax.nn.sigmoid((p['a0'] + ad @ p['a2']).astype(f32))
    g = (jax.nn.sigmoid(gd) @ p['g2']).astype(f32)
    heads = lambda t: t.astype(f32).reshape(B, T, H, N)
    kk = heads(k * p['k_k'])
    kk = kk / jnp.maximum(jnp.linalg.norm(kk, axis=-1, keepdims=True), 1e-12)
    rh, vh, ah, wh = heads(r), heads(v), heads(a), heads(decay)
    kh = heads(k) * (1.0 + (ah - 1.0) * p['k_a'].astype(f32).reshape(H, N))
    bh = kk * ah
    tm = lambda t: jnp.moveaxis(t, 1, 0)
    S_fin, o = lax.scan(wkv_step, S0.astype(f32), (tm(rh), tm(wh), tm(kh), tm(vh), tm(kk), tm(bh)))
    o = jnp.moveaxis(o, 0, 1)
    mu = jnp.mean(o, axis=-1, keepdims=True)
    var = jnp.mean(jnp.square(o - mu), axis=-1, keepdims=True)
    o = ((o - mu) * lax.rsqrt(var + GN_EPS)).reshape(B, T, D_RWKV)
    o = o * p['gn_g'].astype(f32) + p['gn_b'].astype(f32)
    bonus = jnp.sum(rh * kh * p['r_k'].astype(f32).reshape(H, N), axis=-1, keepdims=True) * vh
    out = ((o + bonus.reshape(B, T, D_RWKV)) * g).astype(p_rwkv.dtype)
    return out, S_fin.astype(S0.dtype), p_rwkv[:, -1:]


def conv_group(ua, ub, conv_prev, p):
    u = ua * jax.nn.sigmoid(ub)
    full = jnp.concatenate([conv_prev.astype(u.dtype), u], axis=1)
    y = lax.conv_general_dilated(full, p['conv_w'][:, None, :].astype(u.dtype), window_strides=(1,),
                                 padding='VALID', dimension_numbers=('NWC', 'WIO', 'NWC'),
                                 feature_group_count=D_CONV)
    y = jax.nn.silu(layer_norm(y + p['conv_b'], p['conv_ln_g'], p['conv_ln_b']))
    return y, full[:, full.shape[1] - (CONV_WIDTH - 1):]


def swiglu(x, wg, wu, wd):
    return (jax.nn.silu(x @ wg) * (x @ wu)) @ wd


def route(x2d, w_router, router_bias):
    n = x2d.shape[0]
    per_group = N_EXPERTS // N_EXPERT_GROUPS
    s = jax.nn.sigmoid((x2d @ w_router).astype(jnp.float32))
    biased = s + router_bias.astype(jnp.float32)
    group_score = jnp.sum(lax.top_k(biased.reshape(n, N_EXPERT_GROUPS, per_group), 2)[0], axis=-1)
    _, top_groups = lax.top_k(group_score, TOPK_GROUPS)
    group_mask = jnp.sum(jax.nn.one_hot(top_groups, N_EXPERT_GROUPS, dtype=jnp.float32), axis=1) > 0
    expert_mask = jnp.repeat(group_mask, per_group, axis=-1)
    _, idx = lax.top_k(jnp.where(expert_mask, biased, -jnp.inf), TOP_K)
    sel = jnp.take_along_axis(s, idx, axis=-1)
    gates = sel / jnp.sum(sel, axis=-1, keepdims=True) * ROUTED_SCALE
    return idx, gates


def routed_experts(x2d, idx, gates, w_gate, w_up, w_down):
    n, k = idx.shape
    n_rows = n * k
    blk = 128 if n_rows >= 128 * N_EXPERTS else 8
    n_blocks = -(-(n_rows + N_EXPERTS * (blk - 1)) // blk)
    flat_e = idx.reshape(-1)
    flat_tok = jnp.arange(n_rows, dtype=jnp.int32) // k
    flat_g = gates.reshape(-1)
    order = jnp.argsort(flat_e)
    se, stok, sg = flat_e[order], flat_tok[order], flat_g[order]
    counts = jnp.bincount(flat_e, length=N_EXPERTS)
    start = jnp.cumsum(counts) - counts
    padded = (counts + blk - 1) // blk * blk
    pend = jnp.cumsum(padded)
    pstart = pend - padded
    dest = pstart[se] + (jnp.arange(n_rows, dtype=jnp.int32) - start[se])
    tok_buf = jnp.zeros((n_blocks * blk,), jnp.int32).at[dest].set(stok)
    gate_buf = jnp.zeros((n_blocks * blk,), jnp.float32).at[dest].set(sg)
    blk_expert = jnp.minimum(jnp.searchsorted(pend, jnp.arange(n_blocks, dtype=jnp.int32) * blk, side='right'),
                             N_EXPERTS - 1)

    def block(args):
        tok, g, e = args
        xb = x2d[tok]
        hb = jax.nn.silu(xb @ w_gate[e]) * (xb @ w_up[e])
        return (hb @ w_down[e]).astype(jnp.float32) * g[:, None]

    out = lax.map(block, (tok_buf.reshape(n_blocks, blk), gate_buf.reshape(n_blocks, blk), blk_expert))
    return jax.ops.segment_sum(out.reshape(n_blocks * blk, -1), tok_buf, num_segments=n)


def moe(h, p):
    B, T, D = h.shape
    x2d = h.reshape(B * T, D)
    idx, gates = route(x2d, p['w_router'], p['router_bias'])
    routed = routed_experts(x2d, idx, gates, p['exp_w_gate'], p['exp_w_up'], p['exp_w_down'])
    shared = swiglu(x2d, p['sh_w_gate'], p['sh_w_up'], p['sh_w_down']).astype(jnp.float32)
    return (routed + shared).astype(h.dtype).reshape(B, T, D)


def encoder_layer(x, S0, shift0, conv0, p):
    proj = x @ p['w_in']
    o_rwkv, S1, shift1 = rwkv7_group(proj[..., :D_SHIFT], shift0, S0, p)
    o_conv, conv1 = conv_group(proj[..., D_SHIFT:D_SHIFT + D_CONV], proj[..., D_SHIFT + D_CONV:], conv0, p)
    mix = jnp.concatenate([o_rwkv, o_conv], axis=-1) @ p['w_out']
    h = layer_norm(ALPHA * x + mix, p['ln1_g'], p['ln1_b'])
    y = layer_norm(ALPHA * h + moe(h, p), p['ln2_g'], p['ln2_b'])
    return y, S1, shift1, conv1


def setup_inputs(seed: int = 0) -> dict:
    key = jax.random.key(seed)
    ks = iter(jax.random.split(key, 48))
    L = DEPTH

    def nrm(shape, scale):
        return jax.random.normal(next(ks), shape, jnp.float32) * scale

    def uni(shape, lo, hi):
        return jax.random.uniform(next(ks), shape, jnp.float32, lo, hi)

    return {
        'x_prompt': nrm((BATCH, SEQ, D_MODEL), 1.0),
        'x_sample': nrm((DEC_BATCH, DEC_SEQ, D_MODEL), 1.0),
        'state_wkv': nrm((L, DEC_BATCH, N_RWKV_HEADS, HEAD_DIM, HEAD_DIM), 0.5),
        'state_shift': nrm((L, DEC_BATCH, 1, D_SHIFT), 1.0),
        'state_conv': nrm((L, DEC_BATCH, CONV_WIDTH - 1, D_CONV), 0.5),
        'w_in': nrm((L, D_MODEL, D_IN), D_MODEL ** -0.5),
        'mu_shift': uni((L, D_SHIFT), 0.0, 1.0),
        'w0': uni((L, D_RWKV), -5.0, 1.0),
        'w2': nrm((L, LORA_DECAY, D_RWKV), LORA_DECAY ** -0.5),
        'a0': nrm((L, D_RWKV), 0.1),
        'a2': nrm((L, LORA_ICLR, D_RWKV), LORA_ICLR ** -0.5),
        'g2': nrm((L, LORA_GATE, D_RWKV), LORA_GATE ** -0.5),
        'k_k': 0.85 + nrm((L, D_RWKV), 0.05),
        'k_a': 1.0 + nrm((L, D_RWKV), 0.05),
        'r_k': nrm((L, D_RWKV), 0.1),
        'gn_g': 1.0 + nrm((L, D_RWKV), 0.05),
        'gn_b': nrm((L, D_RWKV), 0.02),
        'conv_w': nrm((L, CONV_WIDTH, D_CONV), CONV_WIDTH ** -0.5),
        'conv_b': nrm((L, D_CONV), 0.02),
        'conv_ln_g': 1.0 + nrm((L, D_CONV), 0.05),
        'conv_ln_b': nrm((L, D_CONV), 0.02),
        'w_out': nrm((L, D_MIX, D_MODEL), D_MIX ** -0.5 * BETA),
        'ln1_g': 1.0 + nrm((L, D_MODEL), 0.05),
        'ln1_b': nrm((L, D_MODEL), 0.02),
        'w_router': nrm((L, D_MODEL, N_EXPERTS), D_MODEL ** -0.5),
        'router_bias': nrm((L, N_EXPERTS), 0.01),
        'exp_w_gate': nrm((L, N_EXPERTS, D_MODEL, D_EXPERT), D_MODEL ** -0.5),
        'exp_w_up': nrm((L, N_EXPERTS, D_MODEL, D_EXPERT), D_MODEL ** -0.5),
        'exp_w_down': nrm((L, N_EXPERTS, D_EXPERT, D_MODEL), D_EXPERT ** -0.5 * BETA),
        'sh_w_gate': nrm((L, D_MODEL, D_SHARED), D_MODEL ** -0.5),
        'sh_w_up': nrm((L, D_MODEL, D_SHARED), D_MODEL ** -0.5),
        'sh_w_down': nrm((L, D_SHARED, D_MODEL), D_SHARED ** -0.5 * BETA),
        'ln2_g': 1.0 + nrm((L, D_MODEL), 0.05),
        'ln2_b': nrm((L, D_MODEL), 0.02),
    }


def reference(x_prompt, x_sample, state_wkv, state_shift, state_conv, w_in, mu_shift, w0, w2, a0, a2, g2,
              k_k, k_a, r_k, gn_g, gn_b, conv_w, conv_b, conv_ln_g, conv_ln_b, w_out, ln1_g, ln1_b,
              w_router, router_bias, exp_w_gate, exp_w_up, exp_w_down, sh_w_gate, sh_w_up, sh_w_down,
              ln2_g, ln2_b):
    yp, ys = x_prompt, x_sample
    bp, dt = x_prompt.shape[0], x_prompt.dtype
    wkv_p, shift_p, conv_p, wkv_s, shift_s, conv_s = [], [], [], [], [], []
    for l in range(DEPTH):
        p = {'w_in': w_in[l], 'mu_shift': mu_shift[l], 'w0': w0[l], 'w2': w2[l], 'a0': a0[l], 'a2': a2[l],
             'g2': g2[l], 'k_k': k_k[l], 'k_a': k_a[l], 'r_k': r_k[l], 'gn_g': gn_g[l], 'gn_b': gn_b[l],
             'conv_w': conv_w[l], 'conv_b': conv_b[l], 'conv_ln_g': conv_ln_g[l], 'conv_ln_b': conv_ln_b[l],
             'w_out': w_out[l], 'ln1_g': ln1_g[l], 'ln1_b': ln1_b[l], 'w_router': w_router[l],
             'router_bias': router_bias[l], 'exp_w_gate': exp_w_gate[l], 'exp_w_up': exp_w_up[l],
             'exp_w_down': exp_w_down[l], 'sh_w_gate': sh_w_gate[l], 'sh_w_up': sh_w_up[l],
             'sh_w_down': sh_w_down[l], 'ln2_g': ln2_g[l], 'ln2_b': ln2_b[l]}
        S0 = jnp.zeros((bp, N_RWKV_HEADS, HEAD_DIM, HEAD_DIM), dt)
        sh0 = jnp.zeros((bp, 1, D_SHIFT), dt)
        cv0 = jnp.zeros((bp, CONV_WIDTH - 1, D_CONV), dt)
        yp, S1, sh1, cv1 = encoder_layer(yp, S0, sh0, cv0, p)
        wkv_p.append(S1)
        shift_p.append(sh1)
        conv_p.append(cv1)
        ys, S2, sh2, cv2 = encoder_layer(ys, state_wkv[l], state_shift[l], state_conv[l], p)
        wkv_s.append(S2)
        shift_s.append(sh2)
        conv_s.append(cv2)
    return (yp, ys, jnp.stack(wkv_p), jnp.stack(shift_p), jnp.stack(conv_p),
            jnp.stack(wkv_s), jnp.stack(shift_s), jnp.stack(conv_s))
```

```python
import functools
import math

import jax
import jax.numpy as jnp
from jax import lax
from jax.experimental import pallas as pl
from jax.experimental.pallas import tpu as pltpu

F32 = jnp.float32
BF16 = jnp.bfloat16
U32 = jnp.uint32
I32 = jnp.int32
HIGHEST = lax.Precision.HIGHEST

D_MODEL = 2048
D_RWKV = 1024
D_CONV = 1024
HEAD_DIM = 64
N_HEADS = D_RWKV // HEAD_DIM
N_PAIRS = N_HEADS // 2
CONV_WIDTH = 31
D_SHIFT = 3 * D_RWKV + 64 + 64 + 128
N_EXPERTS = 256
TOP_K = 8
N_GROUPS = 8
GROUP_SIZE = N_EXPERTS // N_GROUPS
TOPK_GROUPS = 4
D_EXPERT = 576
ROUTED_SCALE = 2.5
ALPHA = 2.0 ** 0.25
LN_EPS = 1e-5
GN_EPS = 64e-5

LANES = 128
ROW_BLK = 128
WORDS = D_MODEL // 2 // LANES
SLABS = D_MODEL // LANES
N_BLOCKS_MAX = lambda n_rows: -(-(n_rows + N_EXPERTS * (ROW_BLK - 1)) // ROW_BLK)


def _cparams(sem, vmem_mb=48):
    return pltpu.CompilerParams(dimension_semantics=sem, vmem_limit_bytes=vmem_mb * 2 ** 20)


def _sigmoid(x):
    return 1.0 / (1.0 + jnp.exp(-x))


def _dot(a, b):
    return jnp.dot(a.astype(BF16), b.astype(BF16), preferred_element_type=F32)


def _dot_nt(a, b):
    return lax.dot_general(a.astype(BF16), b.astype(BF16), (((1,), (1,)), ((), ())),
                           preferred_element_type=F32)


def _dot_tn(a, b):
    return lax.dot_general(a.astype(BF16), b.astype(BF16), (((0,), (0,)), ((), ())),
                           preferred_element_type=F32)


def _head_ones():
    r = lax.broadcasted_iota(I32, (LANES, LANES), 0) // HEAD_DIM
    c = lax.broadcasted_iota(I32, (LANES, LANES), 1) // HEAD_DIM
    return (r == c).astype(F32)


def _segsum(x, ones_bd):
    outs = [jnp.dot(x[:, s * LANES:(s + 1) * LANES], ones_bd, precision=HIGHEST,
                    preferred_element_type=F32) for s in range(x.shape[1] // LANES)]
    return outs[0] if len(outs) == 1 else jnp.concatenate(outs, axis=1)


def _layer_norm(x, g, b):
    mu = jnp.mean(x, axis=-1, keepdims=True)
    d = x - mu
    var = jnp.mean(d * d, axis=-1, keepdims=True)
    return d * lax.rsqrt(var + LN_EPS) * g + b


def _mm_kernel(x_ref, w_ref, o_ref):
    o_ref[...] = jnp.dot(x_ref[...].astype(BF16), w_ref[...], preferred_element_type=F32)


def _matmul(x, w_bf, tm, tn, name):
    m, k = x.shape
    n = w_bf.shape[1]
    return pl.pallas_call(
        _mm_kernel, grid=(n // tn, m // tm),
        in_specs=[pl.BlockSpec((tm, k), lambda j, i: (i, 0)),
                  pl.BlockSpec((k, tn), lambda j, i: (0, j))],
        out_specs=pl.BlockSpec((tm, tn), lambda j, i: (i, j)),
        out_shape=jax.ShapeDtypeStruct((m, n), F32),
        compiler_params=_cparams(("arbitrary", "arbitrary")), name=name)(x, w_bf)


def _glu_kernel(x_ref, wa_ref, wb_ref, o_ref):
    x = x_ref[...].astype(BF16)
    a = jnp.dot(x, wa_ref[...], preferred_element_type=F32)
    b = jnp.dot(x, wb_ref[...], preferred_element_type=F32)
    o_ref[...] = a * _sigmoid(b)


def _glu_matmul(x, wa_bf, wb_bf, tm, name):
    m, k = x.shape
    n = wa_bf.shape[1]
    return pl.pallas_call(
        _glu_kernel, grid=(m // tm,),
        in_specs=[pl.BlockSpec((tm, k), lambda i: (i, 0)),
                  pl.BlockSpec((k, n), lambda i: (0, 0)),
                  pl.BlockSpec((k, n), lambda i: (0, 0))],
        out_specs=pl.BlockSpec((tm, n), lambda i: (i, 0)),
        out_shape=jax.ShapeDtypeStruct((m, n), F32),
        compiler_params=_cparams(("arbitrary",)), name=name)(x, wa_bf, wb_bf)


def _prep_kernel(p_ref, pv_ref, sh_ref, mu_ref, w0_ref, a0_ref, kk_ref, ka_ref, rk_ref,
                 w2_ref, a2_ref, g2_ref,
                 r_out, lw_out, k_out, v_out, kn_out, b_out, g_out, bonus_out):
    i = pl.program_id(1)
    p = p_ref[...]
    tt = p.shape[0]
    carry = jnp.where(i == 0, sh_ref[...], pv_ref[7:8, :])
    row = lax.broadcasted_iota(I32, (tt, 1), 0)
    prev = jnp.where(row == 0, carry, pltpu.roll(p, 1, 0))
    xs = p + (prev - p) * mu_ref[...]
    r = xs[:, 0:D_RWKV]
    k = xs[:, D_RWKV:2 * D_RWKV]
    v = xs[:, 2 * D_RWKV:3 * D_RWKV]
    wa = xs[:, 3 * D_RWKV:3 * D_RWKV + 128]
    gd = xs[:, 3 * D_RWKV + 128:]
    z = w0_ref[...] + _dot(jnp.tanh(wa), w2_ref[...])
    w_log = -(jnp.maximum(-z, 0.0) + jnp.log1p(jnp.exp(-jnp.abs(z)))) - 0.5
    lw_out[...] = -jnp.exp(w_log)
    a = _sigmoid(a0_ref[...] + _dot(wa, a2_ref[...]))
    g_out[...] = _dot(_sigmoid(gd), g2_ref[...])
    ones_bd = _head_ones()
    kk = k * kk_ref[...]
    nrm = jnp.sqrt(_segsum(kk * kk, ones_bd))
    kn = kk / jnp.maximum(nrm, 1e-12)
    kh = k * (1.0 + (a - 1.0) * ka_ref[...])
    r_out[...] = r
    k_out[...] = kh
    v_out[...] = v
    kn_out[...] = kn
    b_out[...] = kn * a
    bonus_out[...] = _segsum(r * kh * rk_ref[...], ones_bd) * v


def _rwkv_prep(proj, shift0, vecs, mats, n_seq, seq_len, tt):
    nt = seq_len // tt
    n = n_seq * seq_len
    row_vec = lambda c: pl.BlockSpec((1, c), lambda b, i: (0, 0))
    full = lambda a: pl.BlockSpec(a.shape, lambda b, i: (0, 0))
    out_spec = pl.BlockSpec((tt, D_RWKV), lambda b, i: (b * nt + i, 0))
    outs = pl.pallas_call(
        _prep_kernel, grid=(n_seq, nt),
        in_specs=[pl.BlockSpec((tt, D_SHIFT), lambda b, i: (b * nt + i, 0)),
                  pl.BlockSpec((8, D_SHIFT), lambda b, i: (jnp.maximum((b * nt + i) * (tt // 8) - 1, 0), 0)),
                  pl.BlockSpec((None, 1, D_SHIFT), lambda b, i: (b, 0, 0)),
                  row_vec(D_SHIFT)] + [row_vec(D_RWKV)] * 5 + [full(m) for m in mats],
        out_specs=[out_spec] * 8,
        out_shape=[jax.ShapeDtypeStruct((n, D_RWKV), F32)] * 8,
        compiler_params=_cparams(("arbitrary", "arbitrary")), name="rwkv_prep")(
            proj, proj, shift0, *vecs, *mats)
    return outs


def _wkv_kernel(r_ref, lw_ref, k_ref, v_ref, kn_ref, b_ref, g_ref, bonus_ref, gng_ref, gnb_ref, s0_ref,
                o_ref, sfin_ref, s_scr):
    c = pl.program_id(1)
    nc = pl.num_programs(1)
    C = r_ref.shape[0]

    @pl.when(c == 0)
    def _():
        s_scr[...] = s0_ref[...]

    lane = lax.broadcasted_iota(I32, (C, LANES), 1)
    m0 = lane < HEAD_DIM
    ri = lax.broadcasted_iota(I32, (2 * C, 2 * C), 0)
    ci = lax.broadcasted_iota(I32, (2 * C, 2 * C), 1)
    same = (ri // C) == (ci // C)
    strict = same & (ri > ci)
    incl = same & (ri >= ci)
    eye = (ri == ci).astype(F32)
    tr = lax.broadcasted_iota(I32, (C, C), 0)
    tc = lax.broadcasted_iota(I32, (C, C), 1)
    ltri = (tr >= tc).astype(F32)
    ones_bd = _head_ones()
    n_sq = int(math.log2(C)) - 1

    def stack(x):
        return jnp.concatenate([jnp.where(m0, x, 0.0), jnp.where(m0, 0.0, x)], axis=0)

    for p in range(N_PAIRS):
        sl = slice(p * LANES, (p + 1) * LANES)
        lw = lw_ref[:, sl]
        lc = jnp.dot(ltri, lw, precision=HIGHEST, preferred_element_type=F32)
        lc_end = lc[C - 1:C, :]
        e_pos = jnp.exp(lc)
        e_neg = jnp.exp(-lc)
        e_end = jnp.exp(lc_end - lc)
        kn = kn_ref[:, sl]
        bb = b_ref[:, sl]
        kh = k_ref[:, sl]
        a_st = stack(-kn * jnp.exp(lc - lw))
        r_st = stack(r_ref[:, sl] * e_pos)
        b_st = stack(bb * e_neg)
        k_st = stack(kh * e_neg)
        v_st = stack(v_ref[:, sl])
        bend_st = stack(bb * e_end)
        kend_st = stack(kh * e_end)
        ab = jnp.where(strict, _dot_nt(a_st, b_st), 0.0)
        ak = jnp.where(strict, _dot_nt(a_st, k_st), 0.0)
        rb = jnp.where(incl, _dot_nt(r_st, b_st), 0.0)
        rk = jnp.where(incl, _dot_nt(r_st, k_st), 0.0)
        tm = eye + ab
        pw = ab
        for _ in range(n_sq):
            pw = _dot(pw, pw)
            tm = tm + _dot(tm, pw)
        s = s_scr[p]
        u = _dot(tm, _dot_nt(a_st, s) + _dot(ak, v_st))
        o_st = _dot_nt(r_st, s) + _dot(rb, u) + _dot(rk, v_st)
        o = o_st[:C] + o_st[C:]
        s_new = s * jnp.exp(lc_end) + _dot_tn(u, bend_st) + _dot_tn(v_st, kend_st)
        s_scr[p] = s_new
        mu = _segsum(o, ones_bd) * (1.0 / HEAD_DIM)
        d = o - mu
        var = _segsum(d * d, ones_bd) * (1.0 / HEAD_DIM)
        on = d * lax.rsqrt(var + GN_EPS) * gng_ref[:, sl] + gnb_ref[:, sl]
        o_ref[:, sl] = (on + bonus_ref[:, sl]) * g_ref[:, sl]

    @pl.when(c == nc - 1)
    def _():
        sfin_ref[...] = s_scr[...]


def _wkv(streams, gn_g, gn_b, s0_bd, n_seq, seq_len, chunk):
    nc = seq_len // chunk
    n = n_seq * seq_len
    tok = pl.BlockSpec((chunk, D_RWKV), lambda b, c: (b * nc + c, 0))
    vec = pl.BlockSpec((1, D_RWKV), lambda b, c: (0, 0))
    st = pl.BlockSpec((None, N_PAIRS, LANES, LANES), lambda b, c: (b, 0, 0, 0))
    return pl.pallas_call(
        _wkv_kernel, grid=(n_seq, nc),
        in_specs=[tok] * 8 + [vec, vec, st],
        out_specs=[tok, st],
        out_shape=[jax.ShapeDtypeStruct((n, D_RWKV), F32),
                   jax.ShapeDtypeStruct((n_seq, N_PAIRS, LANES, LANES), F32)],
        scratch_shapes=[pltpu.VMEM((N_PAIRS, LANES, LANES), F32)],
        compiler_params=_cparams(("arbitrary", "arbitrary")), name="wkv_chunks")(
            *streams, gn_g, gn_b, s0_bd)


def _to_block_diag(s):
    b = s.shape[0]
    s = s.reshape(b, N_PAIRS, 2, HEAD_DIM, HEAD_DIM)
    z = jnp.zeros_like(s[:, :, 0])
    top = jnp.concatenate([s[:, :, 0], z], axis=-1)
    bot = jnp.concatenate([z, s[:, :, 1]], axis=-1)
    return jnp.concatenate([top, bot], axis=-2)


def _from_block_diag(bd):
    b = bd.shape[0]
    h0 = bd[:, :, :HEAD_DIM, :HEAD_DIM]
    h1 = bd[:, :, HEAD_DIM:, HEAD_DIM:]
    return jnp.stack([h0, h1], axis=2).reshape(b, N_HEADS, HEAD_DIM, HEAD_DIM)


HALO = 32


def _conv_kernel(u_ref, halo_ref, cp_ref, cw_ref, cb_ref, lg_ref, lb_ref, o_ref, xbuf, ybuf):
    i = pl.program_id(1)
    tt = u_ref.shape[0]
    xbuf[0:HALO, :] = jnp.where(i == 0, cp_ref[...], halo_ref[...])
    xbuf[HALO:HALO + tt, :] = u_ref[...]
    for cs in range(D_CONV // LANES):
        sl = slice(cs * LANES, (cs + 1) * LANES)
        acc = jnp.zeros((tt, LANES), F32)
        for r in range(8):
            win = xbuf[8 - r:8 - r + tt + 24, sl]
            for q in range(4):
                s = 8 * q + r
                if s > CONV_WIDTH - 1:
                    continue
                j = CONV_WIDTH - 1 - s
                acc = acc + cw_ref[j:j + 1, sl] * win[24 - 8 * q:24 - 8 * q + tt]
        ybuf[:, sl] = acc + cb_ref[:, sl]
    y = _layer_norm(ybuf[...], lg_ref[...], lb_ref[...])
    o_ref[...] = y * _sigmoid(y)


def _conv_module(u, conv0_pad, cw, cb, lg, lb, n_seq, seq_len, tt):
    nt = seq_len // tt
    n = n_seq * seq_len
    vec = pl.BlockSpec((1, D_CONV), lambda b, i: (0, 0))
    return pl.pallas_call(
        _conv_kernel, grid=(n_seq, nt),
        in_specs=[pl.BlockSpec((tt, D_CONV), lambda b, i: (b * nt + i, 0)),
                  pl.BlockSpec((HALO, D_CONV),
                               lambda b, i: (jnp.maximum((b * seq_len + i * tt) // HALO - 1, 0), 0)),
                  pl.BlockSpec((None, HALO, D_CONV), lambda b, i: (b, 0, 0)),
                  pl.BlockSpec((CONV_WIDTH, D_CONV), lambda b, i: (0, 0)), vec, vec, vec],
        out_specs=pl.BlockSpec((tt, D_CONV), lambda b, i: (b * nt + i, 0)),
        out_shape=jax.ShapeDtypeStruct((n, D_CONV), F32),
        scratch_shapes=[pltpu.VMEM((HALO + tt, D_CONV), F32), pltpu.VMEM((tt, D_CONV), F32)],
        compiler_params=_cparams(("arbitrary", "arbitrary")), name="conv_module")(
            u, u, conv0_pad, cw, cb, lg, lb)


def _pack_rows(h):
    lo = pltpu.bitcast(h[:, :D_MODEL // 2].astype(BF16).astype(F32), U32) >> 16
    hi = pltpu.bitcast(h[:, D_MODEL // 2:].astype(BF16).astype(F32), U32) & jnp.uint32(0xFFFF0000)
    return lo | hi


def _outproj_kernel(*refs, aliased):
    if aliased:
        orw_ref, ocv_ref, x_ref, w_ref, g_ref, b_ref, _, _, h_ref, hp_ref = refs
    else:
        orw_ref, ocv_ref, x_ref, w_ref, g_ref, b_ref, h_ref, hp_ref = refs
    tm = x_ref.shape[0]
    mix = (jnp.dot(orw_ref[...].astype(BF16), w_ref[0:D_RWKV, :], preferred_element_type=F32)
           + jnp.dot(ocv_ref[...].astype(BF16), w_ref[D_RWKV:, :], preferred_element_type=F32))
    h = _layer_norm(ALPHA * x_ref[...] + mix, g_ref[...], b_ref[...])
    h_ref[...] = h
    words = _pack_rows(h)
    for s in range(WORDS):
        hp_ref[pl.ds(s, tm, stride=WORDS), :] = words[:, s * LANES:(s + 1) * LANES]


def _outproj_ln(o_rwkv, o_conv, x, w_bf, g, b, tm, n_total, row_off, prev=None):
    m = x.shape[0]
    blk_off = row_off // tm
    vec = pl.BlockSpec((1, D_MODEL), lambda i: (0, 0))
    in_specs = [pl.BlockSpec((tm, D_RWKV), lambda i: (i, 0)),
                pl.BlockSpec((tm, D_CONV), lambda i: (i, 0)),
                pl.BlockSpec((tm, D_MODEL), lambda i: (i, 0)),
                pl.BlockSpec((D_MODEL, D_MODEL), lambda i: (0, 0)), vec, vec]
    args = [o_rwkv, o_conv, x, w_bf, g, b]
    aliases = {}
    if prev is not None:
        in_specs += [pl.BlockSpec(memory_space=pl.ANY)] * 2
        args += list(prev)
        aliases = {6: 0, 7: 1}
    return pl.pallas_call(
        functools.partial(_outproj_kernel, aliased=prev is not None), grid=(m // tm,),
        in_specs=in_specs,
        out_specs=[pl.BlockSpec((tm, D_MODEL), lambda i: (blk_off + i, 0)),
                   pl.BlockSpec((tm * WORDS, LANES), lambda i: (blk_off + i, 0))],
        out_shape=[jax.ShapeDtypeStruct((n_total, D_MODEL), F32),
                   jax.ShapeDtypeStruct((n_total * WORDS, LANES), U32)],
        input_output_aliases=aliases,
        compiler_params=_cparams(("arbitrary",)), name="outproj_ln")(*args)


def _router_kernel(h_ref, wr_ref, bias_ref, idx_ref, gate_ref, rank_ref, cnt_ref, run_scr):
    i = pl.program_id(0)
    tt = h_ref.shape[0]

    @pl.when(i == 0)
    def _():
        run_scr[...] = jnp.zeros_like(run_scr)

    neg = -jnp.inf
    logits = lax.dot_general(wr_ref[...], h_ref[...].astype(BF16), (((1,), (1,)), ((), ())),
                             preferred_element_type=F32)
    s = _sigmoid(logits)
    biased = s + bias_ref[...]
    g3 = biased.reshape(N_GROUPS, GROUP_SIZE, tt)
    io_g = lax.broadcasted_iota(I32, (N_GROUPS, GROUP_SIZE, tt), 1).astype(F32)
    m1 = jnp.max(g3, axis=1, keepdims=True)
    f1 = jnp.min(jnp.where(g3 == m1, io_g, float(GROUP_SIZE)), axis=1, keepdims=True)
    m2 = jnp.max(jnp.where(io_g == f1, neg, g3), axis=1, keepdims=True)
    score = (m1 + m2).reshape(N_GROUPS, tt)
    io8 = lax.broadcasted_iota(I32, (N_GROUPS, tt), 0).astype(F32)
    gsel = jnp.zeros((N_GROUPS, tt), F32)
    for _ in range(TOPK_GROUPS):
        m = jnp.max(score, axis=0, keepdims=True)
        f = jnp.min(jnp.where(score == m, io8, float(N_GROUPS)), axis=0, keepdims=True)
        hit = io8 == f
        gsel = jnp.where(hit, 1.0, gsel)
        score = jnp.where(hit, neg, score)
    emask = jnp.broadcast_to(gsel.reshape(N_GROUPS, 1, tt), (N_GROUPS, GROUP_SIZE, tt)).reshape(N_EXPERTS, tt)
    masked = jnp.where(emask > 0.0, biased, neg)
    io_e = lax.broadcasted_iota(I32, (N_EXPERTS, tt), 0).astype(F32)
    hits, idxs, sels = [], [], []
    for _ in range(TOP_K):
        m = jnp.max(masked, axis=0, keepdims=True)
        f = jnp.min(jnp.where(masked == m, io_e, float(N_EXPERTS)), axis=0, keepdims=True)
        hit = io_e == f
        hits.append(hit)
        idxs.append(f)
        sels.append(jnp.sum(jnp.where(hit, s, 0.0), axis=0, keepdims=True))
        masked = jnp.where(hit, neg, masked)
    denom = sels[0]
    for k in range(1, TOP_K):
        denom = denom + sels[k]
    chosen = jnp.zeros((N_EXPERTS, tt), F32)
    for hit in hits:
        chosen = jnp.where(hit, 1.0, chosen)
    ur = lax.broadcasted_iota(I32, (tt, tt), 0)
    uc = lax.broadcasted_iota(I32, (tt, tt), 1)
    upper = (ur < uc).astype(BF16)
    before = run_scr[...] + jnp.dot(chosen.astype(BF16), upper, preferred_element_type=F32)
    ranks = [jnp.sum(jnp.where(hit, before, 0.0), axis=0, keepdims=True) for hit in hits]
    idx_ref[...] = jnp.concatenate(idxs, axis=0).astype(I32)
    gate_ref[...] = jnp.concatenate([sk / denom * ROUTED_SCALE for sk in sels], axis=0)
    rank_ref[...] = jnp.concatenate(ranks, axis=0).astype(I32)
    run = run_scr[...] + jnp.sum(chosen, axis=1, keepdims=True)
    run_scr[...] = run
    cnt_ref[...] = jnp.broadcast_to(run, (N_EXPERTS, LANES)).astype(I32)


def _router(h_all, wr_t_bf, bias_col, tt):
    n = h_all.shape[0]
    tokrow = pl.BlockSpec((TOP_K, tt), lambda i: (0, i))
    return pl.pallas_call(
        _router_kernel, grid=(n // tt,),
        in_specs=[pl.BlockSpec((tt, D_MODEL), lambda i: (i, 0)),
                  pl.BlockSpec((N_EXPERTS, D_MODEL), lambda i: (0, 0)),
                  pl.BlockSpec((N_EXPERTS, 1), lambda i: (0, 0))],
        out_specs=[tokrow, tokrow, tokrow, pl.BlockSpec((N_EXPERTS, LANES), lambda i: (0, 0))],
        out_shape=[jax.ShapeDtypeStruct((TOP_K, n), I32), jax.ShapeDtypeStruct((TOP_K, n), F32),
                   jax.ShapeDtypeStruct((TOP_K, n), I32), jax.ShapeDtypeStruct((N_EXPERTS, LANES), I32)],
        scratch_shapes=[pltpu.VMEM((N_EXPERTS, 1), F32)],
        compiler_params=_cparams(("arbitrary",)), name="router")(h_all, wr_t_bf, bias_col)


def _dispatch_kernel(dest_ref, hp_ref, xs_ref, sem):
    i = pl.program_id(0)
    tt = dest_ref.shape[1]
    base = i * tt

    def row_copy(tok, dst):
        return pltpu.make_async_copy(hp_ref.at[pl.ds(pl.multiple_of(tok * WORDS, WORDS), WORDS), :],
                                     xs_ref.at[pl.ds(pl.multiple_of(dst * WORDS, WORDS), WORDS), :], sem)

    def issue(t, carry):
        for k in range(TOP_K):
            row_copy(base + t, dest_ref[k, t]).start()
        return carry

    lax.fori_loop(0, tt, issue, 0)

    def drain(t, carry):
        for k in range(TOP_K):
            row_copy(base + t, dest_ref[k, t]).wait()
        return carry

    lax.fori_loop(0, tt, drain, 0)


def _dispatch(dest, hp, n_rows_padded, tt):
    n = dest.shape[1]
    return pl.pallas_call(
        _dispatch_kernel, grid=(n // tt,),
        in_specs=[pl.BlockSpec((TOP_K, tt), lambda i: (0, i), memory_space=pltpu.SMEM),
                  pl.BlockSpec(memory_space=pl.ANY)],
        out_specs=pl.BlockSpec(memory_space=pl.ANY),
        out_shape=jax.ShapeDtypeStruct((n_rows_padded * WORDS, LANES), U32),
        scratch_shapes=[pltpu.SemaphoreType.DMA],
        compiler_params=_cparams(("arbitrary",)), name="dispatch")(dest, hp)


def _experts_kernel(be_ref, nu_ref, xs_ref, wg_ref, wu_ref, wd_ref, ys_ref, wg_s, wu_s, wd_s):
    b = pl.program_id(0)
    live = b < nu_ref[0]
    fresh = (b == 0) | (be_ref[b] != be_ref[jnp.maximum(b - 1, 0)])

    @pl.when(live & fresh)
    def _():
        wg_s[...] = wg_ref[...].astype(BF16)
        wu_s[...] = wu_ref[...].astype(BF16)
        wd_s[...] = wd_ref[...].astype(BF16)

    @pl.when(live)
    def _():
        lo, hi = [], []
        for s in range(WORDS):
            w = xs_ref[pl.ds(s, ROW_BLK, stride=WORDS), :]
            lo.append(pltpu.bitcast(w << 16, F32).astype(BF16))
            hi.append(pltpu.bitcast(w & jnp.uint32(0xFFFF0000), F32).astype(BF16))
        x = jnp.concatenate(lo + hi, axis=1)
        g = jnp.dot(x, wg_s[...], preferred_element_type=F32)
        u = jnp.dot(x, wu_s[...], preferred_element_type=F32)
        hm = (g * _sigmoid(g)) * u
        y = jnp.dot(hm.astype(BF16), wd_s[...], preferred_element_type=F32)
        for s in range(SLABS):
            ys_ref[pl.ds(s, ROW_BLK, stride=SLABS), :] = y[:, s * LANES:(s + 1) * LANES]


def _experts(blk_expert, n_used, xs, wg, wu, wd, n_blocks):
    row_blk = lambda b, be, nu: (jnp.minimum(b, nu[0] - 1), 0)
    wsel = lambda b, be, nu: (be[b], 0, 0)
    grid_spec = pltpu.PrefetchScalarGridSpec(
        num_scalar_prefetch=2, grid=(n_blocks,),
        in_specs=[pl.BlockSpec((ROW_BLK * WORDS, LANES), row_blk),
                  pl.BlockSpec((None, D_MODEL, D_EXPERT), wsel),
                  pl.BlockSpec((None, D_MODEL, D_EXPERT), wsel),
                  pl.BlockSpec((None, D_EXPERT, D_MODEL), wsel)],
        out_specs=pl.BlockSpec((ROW_BLK * SLABS, LANES), row_blk),
        scratch_shapes=[pltpu.VMEM((D_MODEL, D_EXPERT), BF16), pltpu.VMEM((D_MODEL, D_EXPERT), BF16),
                        pltpu.VMEM((D_EXPERT, D_MODEL), BF16)])
    return pl.pallas_call(
        _experts_kernel, grid_spec=grid_spec,
        out_shape=jax.ShapeDtypeStruct((n_blocks * ROW_BLK * SLABS, LANES), F32),
        compiler_params=_cparams(("arbitrary",), vmem_mb=56), name="experts")(
            blk_expert, n_used, xs, wg, wu, wd)


def _combine_kernel(dest_ref, h_ref, gate_ref, swg_ref, swu_ref, swd_ref, g_ref, b_ref, ys_ref,
                    yp_ref, ysm_ref, gbuf, sem, *, n_prompt_tiles):
    i = pl.program_id(0)
    tt = h_ref.shape[0]

    def row_copy(t, k):
        src = dest_ref[k, t]
        return pltpu.make_async_copy(
            ys_ref.at[pl.ds(pl.multiple_of(src * SLABS, SLABS), SLABS), :],
            gbuf.at[pl.ds(pl.multiple_of((k * tt + t) * SLABS, SLABS), SLABS), :], sem)

    def issue(t, carry):
        for k in range(TOP_K):
            row_copy(t, k).start()
        return carry

    lax.fori_loop(0, tt, issue, 0)
    h = h_ref[...]
    x = h.astype(BF16)
    gt = jnp.dot(x, swg_ref[...], preferred_element_type=F32)
    up = jnp.dot(x, swu_ref[...], preferred_element_type=F32)
    shared = jnp.dot(((gt * _sigmoid(gt)) * up).astype(BF16), swd_ref[...], preferred_element_type=F32)

    def drain(t, carry):
        for k in range(TOP_K):
            row_copy(t, k).wait()
        return carry

    lax.fori_loop(0, tt, drain, 0)
    gates = gate_ref[...]
    cols = []
    for s in range(SLABS):
        acc = jnp.zeros((tt, LANES), F32)
        for k in range(TOP_K):
            acc = acc + gates[:, k:k + 1] * gbuf[pl.ds(k * tt * SLABS + s, tt, stride=SLABS), :]
        cols.append(acc)
    routed = jnp.concatenate(cols, axis=1)
    y = _layer_norm(ALPHA * h + (routed + shared), g_ref[...], b_ref[...])

    @pl.when(i < n_prompt_tiles)
    def _():
        yp_ref[...] = y

    @pl.when(i >= n_prompt_tiles)
    def _():
        ysm_ref[...] = y


def _combine(dest, h_all, gates_tok, swg, swu, swd, g, b, ys, n_prompt, tt):
    n = h_all.shape[0]
    npt = n_prompt // tt
    vec = pl.BlockSpec((1, D_MODEL), lambda i: (0, 0))
    full = lambda a: pl.BlockSpec(a.shape, lambda i: (0, 0))
    return pl.pallas_call(
        functools.partial(_combine_kernel, n_prompt_tiles=npt), grid=(n // tt,),
        in_specs=[pl.BlockSpec((TOP_K, tt), lambda i: (0, i), memory_space=pltpu.SMEM),
                  pl.BlockSpec((tt, D_MODEL), lambda i: (i, 0)),
                  pl.BlockSpec((tt, TOP_K), lambda i: (i, 0)),
                  full(swg), full(swu), full(swd), vec, vec,
                  pl.BlockSpec(memory_space=pl.ANY)],
        out_specs=[pl.BlockSpec((tt, D_MODEL), lambda i: (jnp.minimum(i, npt - 1), 0)),
                   pl.BlockSpec((tt, D_MODEL), lambda i: (jnp.maximum(i - npt, 0), 0))],
        out_shape=[jax.ShapeDtypeStruct((n_prompt, D_MODEL), F32),
                   jax.ShapeDtypeStruct((n - n_prompt, D_MODEL), F32)],
        scratch_shapes=[pltpu.VMEM((TOP_K * tt * SLABS, LANES), F32), pltpu.SemaphoreType.DMA],
        compiler_params=_cparams(("arbitrary",)), name="combine_shared_ln")(
            dest, h_all, gates_tok, swg, swu, swd, g, b, ys)


def _mixer(x2d, n_seq, seq_len, s0_bd, shift0, conv0, wts, tm, tn_r, tt_prep, chunk, tt_conv):
    proj_r = _matmul(x2d, wts["w_r"], tm, tn_r, "inproj_rwkv")
    u = _glu_matmul(x2d, wts["w_ca"], wts["w_cb"], tm, "inproj_glu")
    streams = _rwkv_prep(proj_r, shift0, wts["prep_vecs"], wts["prep_mats"], n_seq, seq_len, tt_prep)
    o_rwkv, s_fin = _wkv(streams, wts["gn_g"], wts["gn_b"], s0_bd, n_seq, seq_len, chunk)
    conv0_pad = jnp.pad(conv0, ((0, 0), (HALO - (CONV_WIDTH - 1), 0), (0, 0)))
    o_conv = _conv_module(u, conv0_pad, wts["conv_w"], wts["conv_b"], wts["conv_ln_g"], wts["conv_ln_b"],
                          n_seq, seq_len, tt_conv)
    new_shift = proj_r.reshape(n_seq, seq_len, D_SHIFT)[:, seq_len - 1:, :]
    full = jnp.concatenate([conv0, u.reshape(n_seq, seq_len, D_CONV)], axis=1)
    new_conv = full[:, full.shape[1] - (CONV_WIDTH - 1):]
    return o_rwkv, o_conv, _from_block_diag(s_fin), new_shift, new_conv


def kernel(x_prompt, x_sample, state_wkv, state_shift, state_conv, w_in, mu_shift, w0, w2, a0, a2, g2, k_k, k_a, r_k, gn_g, gn_b, conv_w, conv_b, conv_ln_g, conv_ln_b, w_out, ln1_g, ln1_b, w_router, router_bias, exp_w_gate, exp_w_up, exp_w_down, sh_w_gate, sh_w_up, sh_w_down, ln2_g, ln2_b):
    bp, tp, _ = x_prompt.shape
    bs, ts, _ = x_sample.shape
    n_p, n_s = bp * tp, bs * ts
    n_all = n_p + n_s
    row = lambda v: v.reshape(1, -1)
    zpad = jnp.zeros((64, D_RWKV), F32)
    wts = {
        "w_r": w_in[0][:, :D_SHIFT].astype(BF16),
        "w_ca": w_in[0][:, D_SHIFT:D_SHIFT + D_CONV].astype(BF16),
        "w_cb": w_in[0][:, D_SHIFT + D_CONV:].astype(BF16),
        "prep_vecs": [row(mu_shift[0]), row(w0[0]), row(a0[0]), row(k_k[0]), row(k_a[0]), row(r_k[0])],
        "prep_mats": [jnp.concatenate([w2[0], zpad], axis=0).astype(BF16),
                      jnp.concatenate([zpad, a2[0]], axis=0).astype(BF16),
                      g2[0].astype(BF16)],
        "gn_g": row(gn_g[0]), "gn_b": row(gn_b[0]),
        "conv_w": conv_w[0], "conv_b": row(conv_b[0]),
        "conv_ln_g": row(conv_ln_g[0]), "conv_ln_b": row(conv_ln_b[0]),
    }
    w_out_bf = w_out[0].astype(BF16)
    g1, b1 = row(ln1_g[0]), row(ln1_b[0])

    xp = x_prompt.reshape(n_p, D_MODEL)
    zero_s = jnp.zeros((bp, N_PAIRS, LANES, LANES), F32)
    zero_shift = jnp.zeros((bp, 1, D_SHIFT), F32)
    zero_conv = jnp.zeros((bp, CONV_WIDTH - 1, D_CONV), F32)
    orw_p, ocv_p, wkv_p, shift_p, conv_p = _mixer(xp, bp, tp, zero_s, zero_shift, zero_conv, wts,
                                                  tm=512, tn_r=D_SHIFT // 2, tt_prep=256, chunk=64, tt_conv=128)
    xsm = x_sample.reshape(n_s, D_MODEL)
    orw_s, ocv_s, wkv_s, shift_s, conv_s = _mixer(xsm, bs, ts, _to_block_diag(state_wkv[0]), state_shift[0],
                                                  state_conv[0], wts, tm=n_s, tn_r=D_SHIFT // 2, tt_prep=ts,
                                                  chunk=ts, tt_conv=ts)
    h_all, hp_all = _outproj_ln(orw_p, ocv_p, xp, w_out_bf, g1, b1, 256, n_all, 0)
    h_all, hp_all = _outproj_ln(orw_s, ocv_s, xsm, w_out_bf, g1, b1, n_s, n_all, n_p, prev=(h_all, hp_all))

    tt = ROW_BLK
    idx_t, gate_t, rank_t, cnt = _router(h_all, w_router[0].T.astype(BF16), router_bias[0].reshape(-1, 1), tt)
    counts = cnt[:, 0]
    padded = (counts + ROW_BLK - 1) // ROW_BLK * ROW_BLK
    pend = jnp.cumsum(padded)
    pstart = pend - padded
    dest = pstart[idx_t] + rank_t
    n_blocks = N_BLOCKS_MAX(n_all * TOP_K)
    n_used = (pend[-1] // ROW_BLK).astype(I32).reshape(1)
    blk_start = jnp.arange(n_blocks, dtype=I32) * ROW_BLK
    blk_expert = jnp.minimum(jnp.searchsorted(pend, jnp.minimum(blk_start, pend[-1] - 1), side="right"),
                             N_EXPERTS - 1).astype(I32)
    xs = _dispatch(dest, hp_all, n_blocks * ROW_BLK, tt)
    ys = _experts(blk_expert, n_used, xs, exp_w_gate[0], exp_w_up[0], exp_w_down[0], n_blocks)
    y_p, y_s = _combine(dest, h_all, gate_t.T, sh_w_gate[0].astype(BF16), sh_w_up[0].astype(BF16),
                        sh_w_down[0].astype(BF16), row(ln2_g[0]), row(ln2_b[0]), ys, n_p, tt)
    return (y_p.reshape(bp, tp, D_MODEL), y_s.reshape(bs, ts, D_MODEL),
            wkv_p[None], shift_p[None], conv_p[None], wkv_s[None], shift_s[None], conv_s[None])
```

```python
import functools
import math

import jax
import jax.numpy as jnp
from jax import lax
from jax.experimental import pallas as pl
from jax.experimental.pallas import tpu as pltpu

F32 = jnp.float32
BF16 = jnp.bfloat16
I32 = jnp.int32
HIGHEST = lax.Precision.HIGHEST

D_MODEL = 2048
D_RWKV = 1024
D_CONV = 1024
HEAD_DIM = 64
N_HEADS = D_RWKV // HEAD_DIM
N_PAIRS = N_HEADS // 2
CONV_WIDTH = 31
D_SHIFT = 3 * D_RWKV + 64 + 64 + 128
N_EXPERTS = 256
TOP_K = 8
N_GROUPS = 8
GROUP_SIZE = N_EXPERTS // N_GROUPS
TOPK_GROUPS = 4
D_EXPERT = 576
ROUTED_SCALE = 2.5
ALPHA = 2.0 ** 0.25
LN_EPS = 1e-5
GN_EPS = 64e-5

LANES = 128
ROW_BLK = 128
SLABS = D_MODEL // LANES


def _max_blocks(n_rows):
    return -(-(n_rows + N_EXPERTS * (ROW_BLK - 1)) // ROW_BLK)


def _cparams(sem, vmem_mb=48):
    return pltpu.CompilerParams(dimension_semantics=sem, vmem_limit_bytes=vmem_mb * 2 ** 20)


def _sigmoid(x):
    return 1.0 / (1.0 + jnp.exp(-x))


def _dot(a, b):
    return jnp.dot(a.astype(BF16), b.astype(BF16), preferred_element_type=F32)


def _dot_nt(a, b):
    return lax.dot_general(a.astype(BF16), b.astype(BF16), (((1,), (1,)), ((), ())),
                           preferred_element_type=F32)


def _dot_tn(a, b):
    return lax.dot_general(a.astype(BF16), b.astype(BF16), (((0,), (0,)), ((), ())),
                           preferred_element_type=F32)


def _head_ones():
    r = lax.broadcasted_iota(I32, (LANES, LANES), 0) // HEAD_DIM
    c = lax.broadcasted_iota(I32, (LANES, LANES), 1) // HEAD_DIM
    return (r == c).astype(F32)


def _segsum(x, ones_bd):
    outs = [jnp.dot(x[:, s * LANES:(s + 1) * LANES], ones_bd, precision=HIGHEST,
                    preferred_element_type=F32) for s in range(x.shape[1] // LANES)]
    return outs[0] if len(outs) == 1 else jnp.concatenate(outs, axis=1)


def _layer_norm(x, g, b):
    mu = jnp.mean(x, axis=-1, keepdims=True)
    d = x - mu
    var = jnp.mean(d * d, axis=-1, keepdims=True)
    return d * lax.rsqrt(var + LN_EPS) * g + b


def _mm_kernel(x_ref, w_ref, o_ref):
    o_ref[...] = jnp.dot(x_ref[...].astype(BF16), w_ref[...], preferred_element_type=F32)


def _matmul(x, w_bf, tm, tn, name):
    m, k = x.shape
    n = w_bf.shape[1]
    return pl.pallas_call(
        _mm_kernel, grid=(n // tn, m // tm),
        in_specs=[pl.BlockSpec((tm, k), lambda j, i: (i, 0)),
                  pl.BlockSpec((k, tn), lambda j, i: (0, j))],
        out_specs=pl.BlockSpec((tm, tn), lambda j, i: (i, j)),
        out_shape=jax.ShapeDtypeStruct((m, n), F32),
        compiler_params=_cparams(("arbitrary", "arbitrary")), name=name)(x, w_bf)


def _glu_kernel(x_ref, wa_ref, wb_ref, o_ref):
    x = x_ref[...].astype(BF16)
    a = jnp.dot(x, wa_ref[...], preferred_element_type=F32)
    b = jnp.dot(x, wb_ref[...], preferred_element_type=F32)
    o_ref[...] = a * _sigmoid(b)


def _glu_matmul(x, wa_bf, wb_bf, tm, name):
    m, k = x.shape
    n = wa_bf.shape[1]
    return pl.pallas_call(
        _glu_kernel, grid=(m // tm,),
        in_specs=[pl.BlockSpec((tm, k), lambda i: (i, 0)),
                  pl.BlockSpec((k, n), lambda i: (0, 0)),
                  pl.BlockSpec((k, n), lambda i: (0, 0))],
        out_specs=pl.BlockSpec((tm, n), lambda i: (i, 0)),
        out_shape=jax.ShapeDtypeStruct((m, n), F32),
        compiler_params=_cparams(("arbitrary",)), name=name)(x, wa_bf, wb_bf)


def _prep_kernel(p_ref, pv_ref, sh_ref, mu_ref, w0_ref, a0_ref, kk_ref, ka_ref, rk_ref,
                 w2_ref, a2_ref, g2_ref,
                 r_out, lw_out, k_out, v_out, kn_out, b_out, g_out, bonus_out):
    i = pl.program_id(1)
    p = p_ref[...]
    tt = p.shape[0]
    carry = jnp.where(i == 0, sh_ref[...], pv_ref[7:8, :])
    row = lax.broadcasted_iota(I32, (tt, 1), 0)
    prev = jnp.where(row == 0, carry, pltpu.roll(p, 1, 0))
    xs = p + (prev - p) * mu_ref[...]
    r = xs[:, 0:D_RWKV]
    k = xs[:, D_RWKV:2 * D_RWKV]
    v = xs[:, 2 * D_RWKV:3 * D_RWKV]
    wa = xs[:, 3 * D_RWKV:3 * D_RWKV + 128]
    gd = xs[:, 3 * D_RWKV + 128:]
    z = w0_ref[...] + _dot(jnp.tanh(wa), w2_ref[...])
    w_log = -(jnp.maximum(-z, 0.0) + jnp.log1p(jnp.exp(-jnp.abs(z)))) - 0.5
    lw_out[...] = -jnp.exp(w_log)
    a = _sigmoid(a0_ref[...] + _dot(wa, a2_ref[...]))
    g_out[...] = _dot(_sigmoid(gd), g2_ref[...])
    ones_bd = _head_ones()
    kk = k * kk_ref[...]
    nrm = jnp.sqrt(_segsum(kk * kk, ones_bd))
    kn = kk / jnp.maximum(nrm, 1e-12)
    kh = k * (1.0 + (a - 1.0) * ka_ref[...])
    r_out[...] = r
    k_out[...] = kh
    v_out[...] = v
    kn_out[...] = kn
    b_out[...] = kn * a
    bonus_out[...] = _segsum(r * kh * rk_ref[...], ones_bd) * v


def _rwkv_prep(proj, shift0, vecs, mats, n_seq, seq_len, tt):
    nt = seq_len // tt
    n = n_seq * seq_len
    row_vec = lambda c: pl.BlockSpec((1, c), lambda b, i: (0, 0))
    full = lambda a: pl.BlockSpec(a.shape, lambda b, i: (0, 0))
    out_spec = pl.BlockSpec((tt, D_RWKV), lambda b, i: (b * nt + i, 0))
    outs = pl.pallas_call(
        _prep_kernel, grid=(n_seq, nt),
        in_specs=[pl.BlockSpec((tt, D_SHIFT), lambda b, i: (b * nt + i, 0)),
                  pl.BlockSpec((8, D_SHIFT), lambda b, i: (jnp.maximum((b * nt + i) * (tt // 8) - 1, 0), 0)),
                  pl.BlockSpec((None, 1, D_SHIFT), lambda b, i: (b, 0, 0)),
                  row_vec(D_SHIFT)] + [row_vec(D_RWKV)] * 5 + [full(m) for m in mats],
        out_specs=[out_spec] * 8,
        out_shape=[jax.ShapeDtypeStruct((n, D_RWKV), F32)] * 8,
        compiler_params=_cparams(("arbitrary", "arbitrary")), name="rwkv_prep")(
            proj, proj, shift0, *vecs, *mats)
    return outs


def _wkv_kernel(r_ref, lw_ref, k_ref, v_ref, kn_ref, b_ref, g_ref, bonus_ref, gng_ref, gnb_ref, s0_ref,
                o_ref, sfin_ref, s_scr):
    c = pl.program_id(1)
    nc = pl.num_programs(1)
    nb, C = r_ref.shape[0], r_ref.shape[1]
    C2 = 2 * C
    fused = C2 == LANES
    pairs = range(nb * N_PAIRS)

    @pl.when(c == 0)
    def _():
        s_scr[...] = s0_ref[...]

    lane = lax.broadcasted_iota(I32, (C, LANES), 1)
    m0 = lane < HEAD_DIM
    ri = lax.broadcasted_iota(I32, (C2, C2), 0)
    ci = lax.broadcasted_iota(I32, (C2, C2), 1)
    same = (ri // C) == (ci // C)
    strict = same & (ri > ci)
    incl = same & (ri >= ci)
    eye = (ri == ci).astype(F32)
    tr = lax.broadcasted_iota(I32, (C, C), 0)
    tc = lax.broadcasted_iota(I32, (C, C), 1)
    ltri = (tr >= tc).astype(BF16)
    ones_bd = _head_ones().astype(BF16)
    n_sq = int(math.log2(C)) - 1

    def stack(x):
        return jnp.concatenate([jnp.where(m0, x, 0.0), jnp.where(m0, 0.0, x)], axis=0).astype(BF16)

    def split_dot(lhs_bf, x, rhs_bf, terms):
        acc, rem = None, x
        for _ in range(terms):
            piece = rem.astype(BF16)
            part = (jnp.dot(lhs_bf, piece, preferred_element_type=F32) if rhs_bf is None
                    else jnp.dot(piece, rhs_bf, preferred_element_type=F32))
            acc = part if acc is None else acc + part
            rem = rem - piece.astype(F32)
        return acc

    lw_all = [lw_ref[q] for q in range(nb)]
    lc_all = [split_dot(ltri, lw, None, 3) for lw in lw_all]
    sls = [slice((p % N_PAIRS) * LANES, (p % N_PAIRS + 1) * LANES) for p in pairs]
    seq = [p // N_PAIRS for p in pairs]

    ar, bk, vst, endst, decay_end = [], [], [], [], []
    for p in pairs:
        sl, q = sls[p], seq[p]
        lw, lc = lw_all[q][:, sl], lc_all[q][:, sl]
        lc_end = lc[C - 1:C, :]
        e_neg = jnp.exp(-lc)
        e_end = jnp.exp(lc_end - lc)
        bb, kh = b_ref[q, :, sl], k_ref[q, :, sl]
        ar.append(jnp.concatenate([stack(-kn_ref[q, :, sl] * jnp.exp(lc - lw)),
                                   stack(r_ref[q, :, sl] * jnp.exp(lc))], axis=0))
        bk.append(jnp.concatenate([stack(bb * e_neg), stack(kh * e_neg)], axis=0))
        vst.append(stack(v_ref[q, :, sl]))
        endst.append(jnp.concatenate([stack(bb * e_end), stack(kh * e_end)], axis=0))
        decay_end.append(jnp.exp(lc_end))

    ab, ak, rbk = [], [], []
    for p in pairs:
        if fused:
            m = _dot_nt(ar[p], bk[p])
            ab.append(jnp.where(strict, m[:C2, :C2], 0.0))
            ak.append(jnp.where(strict, m[:C2, C2:], 0.0).astype(BF16))
            rbk.append(jnp.concatenate([jnp.where(incl, m[C2:, :C2], 0.0),
                                        jnp.where(incl, m[C2:, C2:], 0.0)], axis=1).astype(BF16))
        else:
            a_st, r_st, b_st, k_st = ar[p][:C2], ar[p][C2:], bk[p][:C2], bk[p][C2:]
            ab.append(jnp.where(strict, _dot_nt(a_st, b_st), 0.0))
            ak.append(jnp.where(strict, _dot_nt(a_st, k_st), 0.0).astype(BF16))
            rbk.append((jnp.where(incl, _dot_nt(r_st, b_st), 0.0).astype(BF16),
                        jnp.where(incl, _dot_nt(r_st, k_st), 0.0).astype(BF16)))

    tm = [eye + ab[p] for p in pairs]
    pw = [_dot(ab[p], ab[p]) for p in pairs]
    for _ in range(n_sq - 1):
        for p in pairs:
            both = _dot(jnp.concatenate([tm[p], pw[p]], axis=0), pw[p])
            tm[p] = tm[p] + both[:C2]
            pw[p] = both[C2:]
    for p in pairs:
        tm[p] = (tm[p] + _dot(tm[p], pw[p])).astype(BF16)

    s_old = [s_scr[seq[p], p % N_PAIRS] for p in pairs]
    ars = [_dot_nt(ar[p], s_old[p]) for p in pairs]
    rhs = [ars[p][:C2] + jnp.dot(ak[p], vst[p], preferred_element_type=F32) for p in pairs]
    uv = [jnp.concatenate([jnp.dot(tm[p], rhs[p].astype(BF16), preferred_element_type=F32).astype(BF16),
                           vst[p]], axis=0) for p in pairs]
    for p in pairs:
        s_scr[seq[p], p % N_PAIRS] = s_old[p] * decay_end[p] + _dot_tn(uv[p], endst[p])
    for p in pairs:
        sl, q = sls[p], seq[p]
        if fused:
            o_st = ars[p][C2:] + jnp.dot(rbk[p], uv[p], preferred_element_type=F32)
        else:
            o_st = (ars[p][C2:] + jnp.dot(rbk[p][0], uv[p][:C2], preferred_element_type=F32)
                    + jnp.dot(rbk[p][1], uv[p][C2:], preferred_element_type=F32))
        o = o_st[:C] + o_st[C:]
        mu = split_dot(None, o, ones_bd, 2) * (1.0 / HEAD_DIM)
        d = o - mu
        var = split_dot(None, d * d, ones_bd, 2) * (1.0 / HEAD_DIM)
        on = d * lax.rsqrt(var + GN_EPS) * gng_ref[:, sl] + gnb_ref[:, sl]
        o_ref[q, :, sl] = (on + bonus_ref[q, :, sl]) * g_ref[q, :, sl]

    @pl.when(c == nc - 1)
    def _():
        sfin_ref[...] = s_scr[...]


def _wkv(streams, gn_g, gn_b, s0_bd, n_seq, seq_len, chunk, nb):
    nc = seq_len // chunk
    tok = pl.BlockSpec((nb, chunk, D_RWKV), lambda b, c: (b, c, 0))
    vec = pl.BlockSpec((1, D_RWKV), lambda b, c: (0, 0))
    st = pl.BlockSpec((nb, N_PAIRS, LANES, LANES), lambda b, c: (b, 0, 0, 0))
    o, s_fin = pl.pallas_call(
        _wkv_kernel, grid=(n_seq // nb, nc),
        in_specs=[tok] * 8 + [vec, vec, st],
        out_specs=[tok, st],
        out_shape=[jax.ShapeDtypeStruct((n_seq, seq_len, D_RWKV), F32),
                   jax.ShapeDtypeStruct((n_seq, N_PAIRS, LANES, LANES), F32)],
        scratch_shapes=[pltpu.VMEM((nb, N_PAIRS, LANES, LANES), F32)],
        compiler_params=_cparams(("arbitrary", "arbitrary")), name="wkv_chunks")(
            *[s.reshape(n_seq, seq_len, D_RWKV) for s in streams], gn_g, gn_b, s0_bd)
    return o.reshape(n_seq * seq_len, D_RWKV), s_fin


def _to_block_diag(s):
    b = s.shape[0]
    s = s.reshape(b, N_PAIRS, 2, HEAD_DIM, HEAD_DIM)
    z = jnp.zeros_like(s[:, :, 0])
    top = jnp.concatenate([s[:, :, 0], z], axis=-1)
    bot = jnp.concatenate([z, s[:, :, 1]], axis=-1)
    return jnp.concatenate([top, bot], axis=-2)


def _from_block_diag(bd):
    b = bd.shape[0]
    h0 = bd[:, :, :HEAD_DIM, :HEAD_DIM]
    h1 = bd[:, :, HEAD_DIM:, HEAD_DIM:]
    return jnp.stack([h0, h1], axis=2).reshape(b, N_HEADS, HEAD_DIM, HEAD_DIM)


HALO = 32


def _conv_kernel(u_ref, halo_ref, cp_ref, cw_ref, cb_ref, lg_ref, lb_ref, o_ref, xbuf, ybuf):
    i = pl.program_id(1)
    tt = u_ref.shape[0]
    xbuf[0:HALO, :] = jnp.where(i == 0, cp_ref[...], halo_ref[...])
    xbuf[HALO:HALO + tt, :] = u_ref[...]
    for cs in range(D_CONV // LANES):
        sl = slice(cs * LANES, (cs + 1) * LANES)
        acc = jnp.zeros((tt, LANES), F32)
        for r in range(8):
            win = xbuf[8 - r:8 - r + tt + 24, sl]
            for q in range(4):
                s = 8 * q + r
                if s > CONV_WIDTH - 1:
                    continue
                j = CONV_WIDTH - 1 - s
                acc = acc + cw_ref[j:j + 1, sl] * win[24 - 8 * q:24 - 8 * q + tt]
        ybuf[:, sl] = acc + cb_ref[:, sl]
    y = _layer_norm(ybuf[...], lg_ref[...], lb_ref[...])
    o_ref[...] = y * _sigmoid(y)


def _conv_module(u, conv0_pad, cw, cb, lg, lb, n_seq, seq_len, tt):
    nt = seq_len // tt
    n = n_seq * seq_len
    vec = pl.BlockSpec((1, D_CONV), lambda b, i: (0, 0))
    return pl.pallas_call(
        _conv_kernel, grid=(n_seq, nt),
        in_specs=[pl.BlockSpec((tt, D_CONV), lambda b, i: (b * nt + i, 0)),
                  pl.BlockSpec((HALO, D_CONV),
                               lambda b, i: (jnp.maximum((b * seq_len + i * tt) // HALO - 1, 0), 0)),
                  pl.BlockSpec((None, HALO, D_CONV), lambda b, i: (b, 0, 0)),
                  pl.BlockSpec((CONV_WIDTH, D_CONV), lambda b, i: (0, 0)), vec, vec, vec],
        out_specs=pl.BlockSpec((tt, D_CONV), lambda b, i: (b * nt + i, 0)),
        out_shape=jax.ShapeDtypeStruct((n, D_CONV), F32),
        scratch_shapes=[pltpu.VMEM((HALO + tt, D_CONV), F32), pltpu.VMEM((tt, D_CONV), F32)],
        compiler_params=_cparams(("arbitrary", "arbitrary")), name="conv_module")(
            u, u, conv0_pad, cw, cb, lg, lb)


def _outproj_kernel(*refs, aliased):
    if aliased:
        orw_ref, ocv_ref, x_ref, w_ref, g_ref, b_ref, _, _, h_ref, hp_ref = refs
    else:
        orw_ref, ocv_ref, x_ref, w_ref, g_ref, b_ref, h_ref, hp_ref = refs
    tm = x_ref.shape[0]
    mix = (jnp.dot(orw_ref[...].astype(BF16), w_ref[0:D_RWKV, :], preferred_element_type=F32)
           + jnp.dot(ocv_ref[...].astype(BF16), w_ref[D_RWKV:, :], preferred_element_type=F32))
    h = _layer_norm(ALPHA * x_ref[...] + mix, g_ref[...], b_ref[...])
    h_ref[...] = h
    for s in range(SLABS):
        hp_ref[pl.ds(s, tm, stride=SLABS), :] = h[:, s * LANES:(s + 1) * LANES]


def _outproj_ln(o_rwkv, o_conv, x, w_bf, g, b, tm, n_total, row_off, prev=None):
    m = x.shape[0]
    blk_off = row_off // tm
    vec = pl.BlockSpec((1, D_MODEL), lambda i: (0, 0))
    in_specs = [pl.BlockSpec((tm, D_RWKV), lambda i: (i, 0)),
                pl.BlockSpec((tm, D_CONV), lambda i: (i, 0)),
                pl.BlockSpec((tm, D_MODEL), lambda i: (i, 0)),
                pl.BlockSpec((D_MODEL, D_MODEL), lambda i: (0, 0)), vec, vec]
    args = [o_rwkv, o_conv, x, w_bf, g, b]
    aliases = {}
    if prev is not None:
        in_specs += [pl.BlockSpec(memory_space=pl.ANY)] * 2
        args += list(prev)
        aliases = {6: 0, 7: 1}
    return pl.pallas_call(
        functools.partial(_outproj_kernel, aliased=prev is not None), grid=(m // tm,),
        in_specs=in_specs,
        out_specs=[pl.BlockSpec((tm, D_MODEL), lambda i: (blk_off + i, 0)),
                   pl.BlockSpec((tm * SLABS, LANES), lambda i: (blk_off + i, 0))],
        out_shape=[jax.ShapeDtypeStruct((n_total, D_MODEL), F32),
                   jax.ShapeDtypeStruct((n_total * SLABS, LANES), F32)],
        input_output_aliases=aliases,
        compiler_params=_cparams(("arbitrary",)), name="outproj_ln")(*args)


def _router_kernel(h_ref, wr_ref, bias_ref, idx_ref, gate_ref, rank_ref, cnt_ref, run_scr):
    i = pl.program_id(0)
    tt = h_ref.shape[0]

    @pl.when(i == 0)
    def _():
        run_scr[...] = jnp.zeros_like(run_scr)

    neg = -jnp.inf
    logits = lax.dot_general(wr_ref[...], h_ref[...].astype(BF16), (((1,), (1,)), ((), ())),
                             preferred_element_type=F32)
    s = _sigmoid(logits)
    biased = s + bias_ref[...]
    g3 = biased.reshape(N_GROUPS, GROUP_SIZE, tt)
    io_g = lax.broadcasted_iota(I32, (N_GROUPS, GROUP_SIZE, tt), 1).astype(F32)
    m1 = jnp.max(g3, axis=1, keepdims=True)
    f1 = jnp.min(jnp.where(g3 == m1, io_g, float(GROUP_SIZE)), axis=1, keepdims=True)
    m2 = jnp.max(jnp.where(io_g == f1, neg, g3), axis=1, keepdims=True)
    score = (m1 + m2).reshape(N_GROUPS, tt)
    io8 = lax.broadcasted_iota(I32, (N_GROUPS, tt), 0).astype(F32)
    gsel = jnp.zeros((N_GROUPS, tt), F32)
    for _ in range(TOPK_GROUPS):
        m = jnp.max(score, axis=0, keepdims=True)
        f = jnp.min(jnp.where(score == m, io8, float(N_GROUPS)), axis=0, keepdims=True)
        hit = io8 == f
        gsel = jnp.where(hit, 1.0, gsel)
        score = jnp.where(hit, neg, score)
    emask = jnp.broadcast_to(gsel.reshape(N_GROUPS, 1, tt), (N_GROUPS, GROUP_SIZE, tt)).reshape(N_EXPERTS, tt)
    masked = jnp.where(emask > 0.0, biased, neg)
    io_e = lax.broadcasted_iota(I32, (N_EXPERTS, tt), 0).astype(F32)
    hits, idxs, sels = [], [], []
    for _ in range(TOP_K):
        m = jnp.max(masked, axis=0, keepdims=True)
        f = jnp.min(jnp.where(masked == m, io_e, float(N_EXPERTS)), axis=0, keepdims=True)
        hit = io_e == f
        hits.append(hit)
        idxs.append(f)
        sels.append(jnp.sum(jnp.where(hit, s, 0.0), axis=0, keepdims=True))
        masked = jnp.where(hit, neg, masked)
    denom = sels[0]
    for k in range(1, TOP_K):
        denom = denom + sels[k]
    chosen = jnp.zeros((N_EXPERTS, tt), F32)
    for hit in hits:
        chosen = jnp.where(hit, 1.0, chosen)
    ur = lax.broadcasted_iota(I32, (tt, tt), 0)
    uc = lax.broadcasted_iota(I32, (tt, tt), 1)
    upper = (ur < uc).astype(BF16)
    before = run_scr[...] + jnp.dot(chosen.astype(BF16), upper, preferred_element_type=F32)
    ranks = [jnp.sum(jnp.where(hit, before, 0.0), axis=0, keepdims=True) for hit in hits]
    idx_ref[...] = jnp.concatenate(idxs, axis=0).astype(I32)
    gate_ref[...] = jnp.concatenate([sk / denom * ROUTED_SCALE for sk in sels], axis=0)
    rank_ref[...] = jnp.concatenate(ranks, axis=0).astype(I32)
    run = run_scr[...] + jnp.sum(chosen, axis=1, keepdims=True)
    run_scr[...] = run
    cnt_ref[...] = jnp.broadcast_to(run, (N_EXPERTS, LANES)).astype(I32)


def _router(h_all, wr_t_bf, bias_col, tt):
    n = h_all.shape[0]
    tokrow = pl.BlockSpec((TOP_K, tt), lambda i: (0, i))
    return pl.pallas_call(
        _router_kernel, grid=(n // tt,),
        in_specs=[pl.BlockSpec((tt, D_MODEL), lambda i: (i, 0)),
                  pl.BlockSpec((N_EXPERTS, D_MODEL), lambda i: (0, 0)),
                  pl.BlockSpec((N_EXPERTS, 1), lambda i: (0, 0))],
        out_specs=[tokrow, tokrow, tokrow, pl.BlockSpec((N_EXPERTS, LANES), lambda i: (0, 0))],
        out_shape=[jax.ShapeDtypeStruct((TOP_K, n), I32), jax.ShapeDtypeStruct((TOP_K, n), F32),
                   jax.ShapeDtypeStruct((TOP_K, n), I32), jax.ShapeDtypeStruct((N_EXPERTS, LANES), I32)],
        scratch_shapes=[pltpu.VMEM((N_EXPERTS, 1), F32)],
        compiler_params=_cparams(("arbitrary",)), name="router")(h_all, wr_t_bf, bias_col)


def _dispatch_kernel(pstart_ref, idx_ref, rank_ref, hp_ref, xs_ref, sem):
    tt = idx_ref.shape[1]

    def row_copy(src_row, dst_row):
        return pltpu.make_async_copy(hp_ref.at[pl.ds(pl.multiple_of(src_row * SLABS, SLABS), SLABS), :],
                                     xs_ref.at[pl.ds(pl.multiple_of(dst_row * SLABS, SLABS), SLABS), :], sem)

    def issue(t, carry):
        for k in range(TOP_K):
            row_copy(t, pstart_ref[idx_ref[k, t]] + rank_ref[k, t]).start()
        return carry

    def drain(t, carry):
        for k in range(TOP_K):
            row_copy(0, 0).wait()
        return carry

    lax.fori_loop(0, tt, issue, 0)
    lax.fori_loop(0, tt, drain, 0)


def _dispatch(pstart, idx_t, rank_t, hp, n_rows_padded, tt):
    n = idx_t.shape[1]
    smem_tok = pl.BlockSpec((TOP_K, tt), lambda i, ps: (0, i), memory_space=pltpu.SMEM)
    grid_spec = pltpu.PrefetchScalarGridSpec(
        num_scalar_prefetch=1, grid=(n // tt,),
        in_specs=[smem_tok, smem_tok, pl.BlockSpec((tt * SLABS, LANES), lambda i, ps: (i, 0))],
        out_specs=pl.BlockSpec(memory_space=pl.ANY),
        scratch_shapes=[pltpu.SemaphoreType.DMA])
    return pl.pallas_call(
        _dispatch_kernel, grid_spec=grid_spec,
        out_shape=jax.ShapeDtypeStruct((n_rows_padded * SLABS, LANES), F32),
        compiler_params=_cparams(("arbitrary",)), name="dispatch")(pstart, idx_t, rank_t, hp)


def _experts_kernel(be_ref, nu_ref, pend_ref, xs_ref, wg_hbm, wu_hbm, wd_hbm, ys_ref,
                    wg_f, wu_f, wd_f, wg_s, wu_s, wd_s, sems, ord_ref):
    b = pl.program_id(0)
    nu = nu_ref[0]
    last = pl.num_programs(0) - 1
    e = be_ref[b]
    live = b < nu
    fresh = (b == 0) | (e != be_ref[jnp.maximum(b - 1, 0)])
    mats = ((wg_hbm, wg_f, wg_s), (wu_hbm, wu_f, wu_s), (wd_hbm, wd_f, wd_s))

    def fetch(j, expert, slot):
        return pltpu.make_async_copy(mats[j][0].at[expert], mats[j][1].at[slot], sems.at[slot, j])

    nxt1 = pend_ref[e] // ROW_BLK
    e1 = be_ref[jnp.minimum(nxt1, last)]
    nxt2 = jnp.where(nxt1 < nu, pend_ref[e1] // ROW_BLK, nu)
    e2 = be_ref[jnp.minimum(nxt2, last)]

    @pl.when(b == 0)
    def _():
        ord_ref[0] = 0
        for j in range(3):
            fetch(j, e, 0).start()

        @pl.when(nxt1 < nu)
        def _():
            for j in range(3):
                fetch(j, e1, 1).start()

    @pl.when(live & fresh)
    def _():
        slot = ord_ref[0] % 2
        for j in range(3):
            fetch(j, e, slot).wait()
            mats[j][2][...] = mats[j][1][slot].astype(BF16)

            @pl.when(nxt2 < nu)
            def _():
                fetch(j, e2, slot).start()

        ord_ref[0] = ord_ref[0] + 1

    @pl.when(live)
    def _():
        x = jnp.concatenate([xs_ref[pl.ds(s, ROW_BLK, stride=SLABS), :].astype(BF16) for s in range(SLABS)],
                            axis=1)
        nt = (((1,), (1,)), ((), ()))
        g = lax.dot_general(x, wg_s[...], nt, preferred_element_type=F32)
        u = lax.dot_general(x, wu_s[...], nt, preferred_element_type=F32)
        hm = (g * _sigmoid(g)) * u
        y = jnp.dot(hm.astype(BF16), wd_s[...], preferred_element_type=F32)
        for s in range(SLABS):
            ys_ref[pl.ds(s, ROW_BLK, stride=SLABS), :] = y[:, s * LANES:(s + 1) * LANES]


def _experts(blk_expert, n_used, pend, xs, wg_t, wu_t, wd, n_blocks):
    row_blk = lambda b, be, nu, pe: (jnp.minimum(b, nu[0] - 1), 0)
    wshape = (D_EXPERT, D_MODEL)
    grid_spec = pltpu.PrefetchScalarGridSpec(
        num_scalar_prefetch=3, grid=(n_blocks,),
        in_specs=[pl.BlockSpec((ROW_BLK * SLABS, LANES), row_blk)] + [pl.BlockSpec(memory_space=pl.ANY)] * 3,
        out_specs=pl.BlockSpec((ROW_BLK * SLABS, LANES), row_blk),
        scratch_shapes=[pltpu.VMEM((2,) + wshape, F32)] * 3 + [pltpu.VMEM(wshape, BF16)] * 3
        + [pltpu.SemaphoreType.DMA((2, 3)), pltpu.SMEM((1,), I32)])
    return pl.pallas_call(
        _experts_kernel, grid_spec=grid_spec,
        out_shape=jax.ShapeDtypeStruct((n_blocks * ROW_BLK * SLABS, LANES), F32),
        compiler_params=_cparams(("arbitrary",), vmem_mb=56), name="experts")(
            blk_expert, n_used, pend, xs, wg_t, wu_t, wd)


def _combine_kernel(pstart_ref, idx_ref, rank_ref, h_ref, gate_ref, swg_ref, swu_ref, swd_ref, g_ref, b_ref,
                    ys_ref, yp_ref, ysm_ref, gbuf, sem, *, n_prompt_tiles):
    i = pl.program_id(0)
    tt = h_ref.shape[0]

    def row_copy(src_row, dst_row):
        return pltpu.make_async_copy(
            ys_ref.at[pl.ds(pl.multiple_of(src_row * SLABS, SLABS), SLABS), :],
            gbuf.at[pl.ds(pl.multiple_of(dst_row * SLABS, SLABS), SLABS), :], sem)

    def issue(t, carry):
        for k in range(TOP_K):
            row_copy(pstart_ref[idx_ref[k, t]] + rank_ref[k, t], k * tt + t).start()
        return carry

    def drain(t, carry):
        for k in range(TOP_K):
            row_copy(0, 0).wait()
        return carry

    lax.fori_loop(0, tt, issue, 0)
    h = h_ref[...]
    x = h.astype(BF16)
    nt = (((1,), (1,)), ((), ()))
    gt = lax.dot_general(x, swg_ref[...], nt, preferred_element_type=F32)
    up = lax.dot_general(x, swu_ref[...], nt, preferred_element_type=F32)
    shared = jnp.dot(((gt * _sigmoid(gt)) * up).astype(BF16), swd_ref[...], preferred_element_type=F32)
    lax.fori_loop(0, tt, drain, 0)
    gates = gate_ref[...]
    cols = []
    for s in range(SLABS):
        acc = jnp.zeros((tt, LANES), F32)
        for k in range(TOP_K):
            acc = acc + gates[:, k:k + 1] * gbuf[pl.ds(k * tt * SLABS + s, tt, stride=SLABS), :]
        cols.append(acc)
    routed = jnp.concatenate(cols, axis=1)
    y = _layer_norm(ALPHA * h + (routed + shared), g_ref[...], b_ref[...])

    @pl.when(i < n_prompt_tiles)
    def _():
        yp_ref[...] = y

    @pl.when(i >= n_prompt_tiles)
    def _():
        ysm_ref[...] = y


def _combine(pstart, idx_t, rank_t, h_all, gates_tok, swg_t, swu_t, swd, g, b, ys, n_prompt, tt):
    n = h_all.shape[0]
    npt = n_prompt // tt
    vec = pl.BlockSpec((1, D_MODEL), lambda i, ps: (0, 0))
    full = lambda a: pl.BlockSpec(a.shape, lambda i, ps: (0, 0))
    smem_tok = pl.BlockSpec((TOP_K, tt), lambda i, ps: (0, i), memory_space=pltpu.SMEM)
    grid_spec = pltpu.PrefetchScalarGridSpec(
        num_scalar_prefetch=1, grid=(n // tt,),
        in_specs=[smem_tok, smem_tok,
                  pl.BlockSpec((tt, D_MODEL), lambda i, ps: (i, 0)),
                  pl.BlockSpec((tt, TOP_K), lambda i, ps: (i, 0)),
                  full(swg_t), full(swu_t), full(swd), vec, vec,
                  pl.BlockSpec(memory_space=pl.ANY)],
        out_specs=[pl.BlockSpec((tt, D_MODEL), lambda i, ps: (jnp.minimum(i, npt - 1), 0)),
                   pl.BlockSpec((tt, D_MODEL), lambda i, ps: (jnp.maximum(i - npt, 0), 0))],
        scratch_shapes=[pltpu.VMEM((TOP_K * tt * SLABS, LANES), F32), pltpu.SemaphoreType.DMA])
    return pl.pallas_call(
        functools.partial(_combine_kernel, n_prompt_tiles=npt), grid_spec=grid_spec,
        out_shape=[jax.ShapeDtypeStruct((n_prompt, D_MODEL), F32),
                   jax.ShapeDtypeStruct((n - n_prompt, D_MODEL), F32)],
        compiler_params=_cparams(("arbitrary",)), name="combine_shared_ln")(
            pstart, idx_t, rank_t, h_all, gates_tok, swg_t, swu_t, swd, g, b, ys)


def _mixer(x2d, n_seq, seq_len, s0_bd, shift0, conv0, wts, tm, tn_r, tt_prep, chunk, wkv_nb, tt_conv):
    proj_r = _matmul(x2d, wts["w_r"], tm, tn_r, "inproj_rwkv")
    u = _glu_matmul(x2d, wts["w_ca"], wts["w_cb"], tm, "inproj_glu")
    streams = _rwkv_prep(proj_r, shift0, wts["prep_vecs"], wts["prep_mats"], n_seq, seq_len, tt_prep)
    o_rwkv, s_fin = _wkv(streams, wts["gn_g"], wts["gn_b"], s0_bd, n_seq, seq_len, chunk, wkv_nb)
    conv0_pad = jnp.pad(conv0, ((0, 0), (HALO - (CONV_WIDTH - 1), 0), (0, 0)))
    o_conv = _conv_module(u, conv0_pad, wts["conv_w"], wts["conv_b"], wts["conv_ln_g"], wts["conv_ln_b"],
                          n_seq, seq_len, tt_conv)
    new_shift = proj_r.reshape(n_seq, seq_len, D_SHIFT)[:, seq_len - 1:, :]
    full = jnp.concatenate([conv0, u.reshape(n_seq, seq_len, D_CONV)], axis=1)
    new_conv = full[:, full.shape[1] - (CONV_WIDTH - 1):]
    return o_rwkv, o_conv, _from_block_diag(s_fin), new_shift, new_conv


def kernel(x_prompt, x_sample, state_wkv, state_shift, state_conv, w_in, mu_shift, w0, w2, a0, a2, g2, k_k, k_a, r_k, gn_g, gn_b, conv_w, conv_b, conv_ln_g, conv_ln_b, w_out, ln1_g, ln1_b, w_router, router_bias, exp_w_gate, exp_w_up, exp_w_down, sh_w_gate, sh_w_up, sh_w_down, ln2_g, ln2_b):
    bp, tp, _ = x_prompt.shape
    bs, ts, _ = x_sample.shape
    n_p, n_s = bp * tp, bs * ts
    n_all = n_p + n_s
    row = lambda v: v.reshape(1, -1)
    zpad = jnp.zeros((64, D_RWKV), F32)
    wts = {
        "w_r": w_in[0][:, :D_SHIFT].astype(BF16),
        "w_ca": w_in[0][:, D_SHIFT:D_SHIFT + D_CONV].astype(BF16),
        "w_cb": w_in[0][:, D_SHIFT + D_CONV:].astype(BF16),
        "prep_vecs": [row(mu_shift[0]), row(w0[0]), row(a0[0]), row(k_k[0]), row(k_a[0]), row(r_k[0])],
        "prep_mats": [jnp.concatenate([w2[0], zpad], axis=0).astype(BF16),
                      jnp.concatenate([zpad, a2[0]], axis=0).astype(BF16),
                      g2[0].astype(BF16)],
        "gn_g": row(gn_g[0]), "gn_b": row(gn_b[0]),
        "conv_w": conv_w[0], "conv_b": row(conv_b[0]),
        "conv_ln_g": row(conv_ln_g[0]), "conv_ln_b": row(conv_ln_b[0]),
    }
    w_out_bf = w_out[0].astype(BF16)
    g1, b1 = row(ln1_g[0]), row(ln1_b[0])

    xp = x_prompt.reshape(n_p, D_MODEL)
    zero_s = jnp.zeros((bp, N_PAIRS, LANES, LANES), F32)
    zero_shift = jnp.zeros((bp, 1, D_SHIFT), F32)
    zero_conv = jnp.zeros((bp, CONV_WIDTH - 1, D_CONV), F32)
    orw_p, ocv_p, wkv_p, shift_p, conv_p = _mixer(xp, bp, tp, zero_s, zero_shift, zero_conv, wts,
                                                  tm=512, tn_r=D_SHIFT // 2, tt_prep=256, chunk=64, wkv_nb=bp,
                                                  tt_conv=128)
    xsm = x_sample.reshape(n_s, D_MODEL)
    orw_s, ocv_s, wkv_s, shift_s, conv_s = _mixer(xsm, bs, ts, _to_block_diag(state_wkv[0]), state_shift[0],
                                                  state_conv[0], wts, tm=n_s, tn_r=D_SHIFT // 2, tt_prep=ts,
                                                  chunk=ts, wkv_nb=1, tt_conv=ts)
    h_all, hp_all = _outproj_ln(orw_p, ocv_p, xp, w_out_bf, g1, b1, 256, n_all, 0)
    h_all, hp_all = _outproj_ln(orw_s, ocv_s, xsm, w_out_bf, g1, b1, n_s, n_all, n_p, prev=(h_all, hp_all))

    tt = ROW_BLK
    idx_t, gate_t, rank_t, cnt = _router(h_all, w_router[0].T.astype(BF16), router_bias[0].reshape(-1, 1), tt)
    counts = cnt[:, 0]
    padded = (counts + ROW_BLK - 1) // ROW_BLK * ROW_BLK
    pend = jnp.cumsum(padded).astype(I32)
    pstart = pend - padded
    n_blocks = _max_blocks(n_all * TOP_K)
    n_used = (pend[-1:] // ROW_BLK).astype(I32)
    blk_row = jnp.minimum(jnp.arange(n_blocks, dtype=I32) * ROW_BLK, pend[-1] - 1)
    blk_expert = jnp.minimum(jnp.sum((pend[None, :] <= blk_row[:, None]).astype(I32), axis=1), N_EXPERTS - 1)
    xs = _dispatch(pstart, idx_t, rank_t, hp_all, n_blocks * ROW_BLK, tt)
    ys = _experts(blk_expert, n_used, pend, xs, jnp.swapaxes(exp_w_gate[0], 1, 2),
                  jnp.swapaxes(exp_w_up[0], 1, 2), exp_w_down[0], n_blocks)
    y_p, y_s = _combine(pstart, idx_t, rank_t, h_all, gate_t.T, sh_w_gate[0].T.astype(BF16),
                        sh_w_up[0].T.astype(BF16), sh_w_down[0].astype(BF16), row(ln2_g[0]), row(ln2_b[0]),
                        ys, n_p, tt)
    return (y_p.reshape(bp, tp, D_MODEL), y_s.reshape(bs, ts, D_MODEL),
            wkv_p[None], shift_p[None], conv_p[None], wkv_s[None], shift_s[None], conv_s[None])
```

```python
import functools
import math

import jax
import jax.numpy as jnp
from jax import lax
from jax.experimental import pallas as pl
from jax.experimental.pallas import tpu as pltpu

F32 = jnp.float32
BF16 = jnp.bfloat16
I32 = jnp.int32
HIGHEST = lax.Precision.HIGHEST

D_MODEL = 2048
D_RWKV = 1024
D_CONV = 1024
HEAD_DIM = 64
N_HEADS = D_RWKV // HEAD_DIM
N_PAIRS = N_HEADS // 2
CONV_WIDTH = 31
D_SHIFT = 3 * D_RWKV + 64 + 64 + 128
N_EXPERTS = 256
TOP_K = 8
N_GROUPS = 8
GROUP_SIZE = N_EXPERTS // N_GROUPS
TOPK_GROUPS = 4
D_EXPERT = 576
ROUTED_SCALE = 2.5
ALPHA = 2.0 ** 0.25
LN_EPS = 1e-5
GN_EPS = 64e-5

LANES = 128
ROW_BLK = 128
SLABS = D_MODEL // LANES
WEIGHT_DMA_PRIORITY = 1


def _max_blocks(n_rows):
    return -(-(n_rows + N_EXPERTS * (ROW_BLK - 1)) // ROW_BLK)


def _cparams(sem, vmem_mb=48):
    return pltpu.CompilerParams(dimension_semantics=sem, vmem_limit_bytes=vmem_mb * 2 ** 20)


def _sigmoid(x):
    return 1.0 / (1.0 + jnp.exp(-x))


def _dot(a, b):
    return jnp.dot(a.astype(BF16), b.astype(BF16), preferred_element_type=F32)


def _dot_nt(a, b):
    return lax.dot_general(a.astype(BF16), b.astype(BF16), (((1,), (1,)), ((), ())),
                           preferred_element_type=F32)


def _dot_tn(a, b):
    return lax.dot_general(a.astype(BF16), b.astype(BF16), (((0,), (0,)), ((), ())),
                           preferred_element_type=F32)


def _head_ones():
    r = lax.broadcasted_iota(I32, (LANES, LANES), 0) // HEAD_DIM
    c = lax.broadcasted_iota(I32, (LANES, LANES), 1) // HEAD_DIM
    return (r == c).astype(F32)


def _segsum(x, ones_bd):
    outs = [jnp.dot(x[:, s * LANES:(s + 1) * LANES], ones_bd, precision=HIGHEST,
                    preferred_element_type=F32) for s in range(x.shape[1] // LANES)]
    return outs[0] if len(outs) == 1 else jnp.concatenate(outs, axis=1)


def _layer_norm(x, g, b):
    mu = jnp.mean(x, axis=-1, keepdims=True)
    d = x - mu
    var = jnp.mean(d * d, axis=-1, keepdims=True)
    return d * lax.rsqrt(var + LN_EPS) * g + b


def _mm_kernel(x_ref, w_ref, o_ref):
    o_ref[...] = jnp.dot(x_ref[...].astype(BF16), w_ref[...], preferred_element_type=F32)


def _matmul(x, w_bf, tm, tn, name):
    m, k = x.shape
    n = w_bf.shape[1]
    return pl.pallas_call(
        _mm_kernel, grid=(n // tn, m // tm),
        in_specs=[pl.BlockSpec((tm, k), lambda j, i: (i, 0)),
                  pl.BlockSpec((k, tn), lambda j, i: (0, j))],
        out_specs=pl.BlockSpec((tm, tn), lambda j, i: (i, j)),
        out_shape=jax.ShapeDtypeStruct((m, n), F32),
        compiler_params=_cparams(("arbitrary", "arbitrary")), name=name)(x, w_bf)


def _glu_kernel(x_ref, wa_ref, wb_ref, o_ref):
    x = x_ref[...].astype(BF16)
    a = jnp.dot(x, wa_ref[...], preferred_element_type=F32)
    b = jnp.dot(x, wb_ref[...], preferred_element_type=F32)
    o_ref[...] = a * _sigmoid(b)


def _glu_matmul(x, wa_bf, wb_bf, tm, name):
    m, k = x.shape
    n = wa_bf.shape[1]
    return pl.pallas_call(
        _glu_kernel, grid=(m // tm,),
        in_specs=[pl.BlockSpec((tm, k), lambda i: (i, 0)),
                  pl.BlockSpec((k, n), lambda i: (0, 0)),
                  pl.BlockSpec((k, n), lambda i: (0, 0))],
        out_specs=pl.BlockSpec((tm, n), lambda i: (i, 0)),
        out_shape=jax.ShapeDtypeStruct((m, n), F32),
        compiler_params=_cparams(("arbitrary",)), name=name)(x, wa_bf, wb_bf)


def _prep_kernel(p_ref, pv_ref, sh_ref, mu_ref, w0_ref, a0_ref, kk_ref, ka_ref, rk_ref,
                 w2_ref, a2_ref, g2_ref,
                 r_out, lw_out, k_out, v_out, kn_out, b_out, g_out, bonus_out):
    i = pl.program_id(1)
    p = p_ref[...]
    tt = p.shape[0]
    carry = jnp.where(i == 0, sh_ref[...], pv_ref[7:8, :])
    row = lax.broadcasted_iota(I32, (tt, 1), 0)
    prev = jnp.where(row == 0, carry, pltpu.roll(p, 1, 0))
    xs = p + (prev - p) * mu_ref[...]
    r = xs[:, 0:D_RWKV]
    k = xs[:, D_RWKV:2 * D_RWKV]
    v = xs[:, 2 * D_RWKV:3 * D_RWKV]
    wa = xs[:, 3 * D_RWKV:3 * D_RWKV + 128]
    gd = xs[:, 3 * D_RWKV + 128:]
    z = w0_ref[...] + _dot(jnp.tanh(wa), w2_ref[...])
    w_log = -(jnp.maximum(-z, 0.0) + jnp.log1p(jnp.exp(-jnp.abs(z)))) - 0.5
    lw_out[...] = -jnp.exp(w_log)
    a = _sigmoid(a0_ref[...] + _dot(wa, a2_ref[...]))
    g_out[...] = _dot(_sigmoid(gd), g2_ref[...])
    ones_bd = _head_ones()
    kk = k * kk_ref[...]
    nrm = jnp.sqrt(_segsum(kk * kk, ones_bd))
    kn = kk / jnp.maximum(nrm, 1e-12)
    kh = k * (1.0 + (a - 1.0) * ka_ref[...])
    r_out[...] = r
    k_out[...] = kh
    v_out[...] = v
    kn_out[...] = kn
    b_out[...] = kn * a
    bonus_out[...] = _segsum(r * kh * rk_ref[...], ones_bd) * v


def _rwkv_prep(proj, shift0, vecs, mats, n_seq, seq_len, tt):
    nt = seq_len // tt
    n = n_seq * seq_len
    row_vec = lambda c: pl.BlockSpec((1, c), lambda b, i: (0, 0))
    full = lambda a: pl.BlockSpec(a.shape, lambda b, i: (0, 0))
    out_spec = pl.BlockSpec((tt, D_RWKV), lambda b, i: (b * nt + i, 0))
    outs = pl.pallas_call(
        _prep_kernel, grid=(n_seq, nt),
        in_specs=[pl.BlockSpec((tt, D_SHIFT), lambda b, i: (b * nt + i, 0)),
                  pl.BlockSpec((8, D_SHIFT), lambda b, i: (jnp.maximum((b * nt + i) * (tt // 8) - 1, 0), 0)),
                  pl.BlockSpec((None, 1, D_SHIFT), lambda b, i: (b, 0, 0)),
                  row_vec(D_SHIFT)] + [row_vec(D_RWKV)] * 5 + [full(m) for m in mats],
        out_specs=[out_spec] * 8,
        out_shape=[jax.ShapeDtypeStruct((n, D_RWKV), F32)] * 8,
        compiler_params=_cparams(("arbitrary", "arbitrary")), name="rwkv_prep")(
            proj, proj, shift0, *vecs, *mats)
    return outs


def _wkv_kernel(r_ref, lw_ref, k_ref, v_ref, kn_ref, b_ref, g_ref, bonus_ref, gng_ref, gnb_ref, s0_ref,
                o_ref, sfin_ref, s_scr):
    c = pl.program_id(1)
    nc = pl.num_programs(1)
    nb, C = r_ref.shape[0], r_ref.shape[1]
    C2 = 2 * C
    fused = C2 == LANES
    pairs = range(nb * N_PAIRS)

    @pl.when(c == 0)
    def _():
        s_scr[...] = s0_ref[...]

    lane = lax.broadcasted_iota(I32, (C, LANES), 1)
    m0 = lane < HEAD_DIM
    ri = lax.broadcasted_iota(I32, (C2, C2), 0)
    ci = lax.broadcasted_iota(I32, (C2, C2), 1)
    same = (ri // C) == (ci // C)
    strict = same & (ri > ci)
    incl = same & (ri >= ci)
    eye = (ri == ci).astype(F32)
    tr = lax.broadcasted_iota(I32, (C, C), 0)
    tc = lax.broadcasted_iota(I32, (C, C), 1)
    ltri = (tr >= tc).astype(BF16)
    ones_bd = _head_ones().astype(BF16)
    n_sq = int(math.log2(C)) - 1

    def stack(x):
        return jnp.concatenate([jnp.where(m0, x, 0.0), jnp.where(m0, 0.0, x)], axis=0).astype(BF16)

    def split_dot(lhs_bf, x, rhs_bf, terms):
        acc, rem = None, x
        for _ in range(terms):
            piece = rem.astype(BF16)
            part = (jnp.dot(lhs_bf, piece, preferred_element_type=F32) if rhs_bf is None
                    else jnp.dot(piece, rhs_bf, preferred_element_type=F32))
            acc = part if acc is None else acc + part
            rem = rem - piece.astype(F32)
        return acc

    lw_all = [lw_ref[q] for q in range(nb)]
    lc_all = [split_dot(ltri, lw, None, 3) for lw in lw_all]
    sls = [slice((p % N_PAIRS) * LANES, (p % N_PAIRS + 1) * LANES) for p in pairs]
    seq = [p // N_PAIRS for p in pairs]

    ar, bk, vst, endst, decay_end = [], [], [], [], []
    for p in pairs:
        sl, q = sls[p], seq[p]
        lw, lc = lw_all[q][:, sl], lc_all[q][:, sl]
        lc_end = lc[C - 1:C, :]
        e_neg = jnp.exp(-lc)
        e_end = jnp.exp(lc_end - lc)
        bb, kh = b_ref[q, :, sl], k_ref[q, :, sl]
        ar.append(jnp.concatenate([stack(-kn_ref[q, :, sl] * jnp.exp(lc - lw)),
                                   stack(r_ref[q, :, sl] * jnp.exp(lc))], axis=0))
        bk.append(jnp.concatenate([stack(bb * e_neg), stack(kh * e_neg)], axis=0))
        vst.append(stack(v_ref[q, :, sl]))
        endst.append(jnp.concatenate([stack(bb * e_end), stack(kh * e_end)], axis=0))
        decay_end.append(jnp.exp(lc_end))

    ab, ak, rbk = [], [], []
    for p in pairs:
        if fused:
            m = _dot_nt(ar[p], bk[p])
            ab.append(jnp.where(strict, m[:C2, :C2], 0.0))
            ak.append(jnp.where(strict, m[:C2, C2:], 0.0).astype(BF16))
            rbk.append(jnp.concatenate([jnp.where(incl, m[C2:, :C2], 0.0),
                                        jnp.where(incl, m[C2:, C2:], 0.0)], axis=1).astype(BF16))
        else:
            a_st, r_st, b_st, k_st = ar[p][:C2], ar[p][C2:], bk[p][:C2], bk[p][C2:]
            ab.append(jnp.where(strict, _dot_nt(a_st, b_st), 0.0))
            ak.append(jnp.where(strict, _dot_nt(a_st, k_st), 0.0).astype(BF16))
            rbk.append((jnp.where(incl, _dot_nt(r_st, b_st), 0.0).astype(BF16),
                        jnp.where(incl, _dot_nt(r_st, k_st), 0.0).astype(BF16)))

    tm = [eye + ab[p] for p in pairs]
    pw = [_dot(ab[p], ab[p]) for p in pairs]
    for _ in range(n_sq - 1):
        for p in pairs:
            both = _dot(jnp.concatenate([tm[p], pw[p]], axis=0), pw[p])
            tm[p] = tm[p] + both[:C2]
            pw[p] = both[C2:]
    for p in pairs:
        tm[p] = (tm[p] + _dot(tm[p], pw[p])).astype(BF16)

    s_old = [s_scr[seq[p], p % N_PAIRS] for p in pairs]
    ars = [_dot_nt(ar[p], s_old[p]) for p in pairs]
    rhs = [ars[p][:C2] + jnp.dot(ak[p], vst[p], preferred_element_type=F32) for p in pairs]
    uv = [jnp.concatenate([jnp.dot(tm[p], rhs[p].astype(BF16), preferred_element_type=F32).astype(BF16),
                           vst[p]], axis=0) for p in pairs]
    for p in pairs:
        s_scr[seq[p], p % N_PAIRS] = s_old[p] * decay_end[p] + _dot_tn(uv[p], endst[p])
    for p in pairs:
        sl, q = sls[p], seq[p]
        if fused:
            o_st = ars[p][C2:] + jnp.dot(rbk[p], uv[p], preferred_element_type=F32)
        else:
            o_st = (ars[p][C2:] + jnp.dot(rbk[p][0], uv[p][:C2], preferred_element_type=F32)
                    + jnp.dot(rbk[p][1], uv[p][C2:], preferred_element_type=F32))
        o = o_st[:C] + o_st[C:]
        mu = split_dot(None, o, ones_bd, 2) * (1.0 / HEAD_DIM)
        d = o - mu
        var = split_dot(None, d * d, ones_bd, 2) * (1.0 / HEAD_DIM)
        on = d * lax.rsqrt(var + GN_EPS) * gng_ref[:, sl] + gnb_ref[:, sl]
        o_ref[q, :, sl] = (on + bonus_ref[q, :, sl]) * g_ref[q, :, sl]

    @pl.when(c == nc - 1)
    def _():
        sfin_ref[...] = s_scr[...]


def _wkv(streams, gn_g, gn_b, s0_bd, n_seq, seq_len, chunk, nb):
    nc = seq_len // chunk
    tok = pl.BlockSpec((nb, chunk, D_RWKV), lambda b, c: (b, c, 0))
    vec = pl.BlockSpec((1, D_RWKV), lambda b, c: (0, 0))
    st = pl.BlockSpec((nb, N_PAIRS, LANES, LANES), lambda b, c: (b, 0, 0, 0))
    o, s_fin = pl.pallas_call(
        _wkv_kernel, grid=(n_seq // nb, nc),
        in_specs=[tok] * 8 + [vec, vec, st],
        out_specs=[tok, st],
        out_shape=[jax.ShapeDtypeStruct((n_seq, seq_len, D_RWKV), F32),
                   jax.ShapeDtypeStruct((n_seq, N_PAIRS, LANES, LANES), F32)],
        scratch_shapes=[pltpu.VMEM((nb, N_PAIRS, LANES, LANES), F32)],
        compiler_params=_cparams(("arbitrary", "arbitrary")), name="wkv_chunks")(
            *[s.reshape(n_seq, seq_len, D_RWKV) for s in streams], gn_g, gn_b, s0_bd)
    return o.reshape(n_seq * seq_len, D_RWKV), s_fin


def _to_block_diag(s):
    b = s.shape[0]
    s = s.reshape(b, N_PAIRS, 2, HEAD_DIM, HEAD_DIM)
    z = jnp.zeros_like(s[:, :, 0])
    top = jnp.concatenate([s[:, :, 0], z], axis=-1)
    bot = jnp.concatenate([z, s[:, :, 1]], axis=-1)
    return jnp.concatenate([top, bot], axis=-2)


def _from_block_diag(bd):
    b = bd.shape[0]
    h0 = bd[:, :, :HEAD_DIM, :HEAD_DIM]
    h1 = bd[:, :, HEAD_DIM:, HEAD_DIM:]
    return jnp.stack([h0, h1], axis=2).reshape(b, N_HEADS, HEAD_DIM, HEAD_DIM)


HALO = 32


def _conv_kernel(u_ref, halo_ref, cp_ref, cw_ref, cb_ref, lg_ref, lb_ref, o_ref, xbuf, ybuf, wbuf):
    i = pl.program_id(1)
    tt = u_ref.shape[0]
    xbuf[0:HALO, :] = jnp.where(i == 0, cp_ref[...], halo_ref[...])
    xbuf[HALO:HALO + tt, :] = u_ref[...]
    for cs in range(D_CONV // LANES):
        sl = slice(cs * LANES, (cs + 1) * LANES)
        for r in range(8):
            wbuf[r] = xbuf[8 - r:8 - r + tt + 24, sl]
        acc = jnp.zeros((tt, LANES), F32)
        for r in range(8):
            for q in range(4):
                s = 8 * q + r
                if s > CONV_WIDTH - 1:
                    continue
                j = CONV_WIDTH - 1 - s
                acc = acc + cw_ref[j:j + 1, sl] * wbuf[r, 24 - 8 * q:24 - 8 * q + tt, :]
        ybuf[:, sl] = acc + cb_ref[:, sl]
    y = _layer_norm(ybuf[...], lg_ref[...], lb_ref[...])
    o_ref[...] = y * _sigmoid(y)


def _conv_module(u, conv0_pad, cw, cb, lg, lb, n_seq, seq_len, tt):
    nt = seq_len // tt
    n = n_seq * seq_len
    vec = pl.BlockSpec((1, D_CONV), lambda b, i: (0, 0))
    return pl.pallas_call(
        _conv_kernel, grid=(n_seq, nt),
        in_specs=[pl.BlockSpec((tt, D_CONV), lambda b, i: (b * nt + i, 0)),
                  pl.BlockSpec((HALO, D_CONV),
                               lambda b, i: (jnp.maximum((b * seq_len + i * tt) // HALO - 1, 0), 0)),
                  pl.BlockSpec((None, HALO, D_CONV), lambda b, i: (b, 0, 0)),
                  pl.BlockSpec((CONV_WIDTH, D_CONV), lambda b, i: (0, 0)), vec, vec, vec],
        out_specs=pl.BlockSpec((tt, D_CONV), lambda b, i: (b * nt + i, 0)),
        out_shape=jax.ShapeDtypeStruct((n, D_CONV), F32),
        scratch_shapes=[pltpu.VMEM((HALO + tt, D_CONV), F32), pltpu.VMEM((tt, D_CONV), F32),
                        pltpu.VMEM((8, tt + 24, LANES), F32)],
        compiler_params=_cparams(("arbitrary", "arbitrary")), name="conv_module")(
            u, u, conv0_pad, cw, cb, lg, lb)


def _outproj_kernel(*refs, aliased):
    if aliased:
        orw_ref, ocv_ref, x_ref, w_ref, g_ref, b_ref, _, _, h_ref, hp_ref, tile_scr = refs
    else:
        orw_ref, ocv_ref, x_ref, w_ref, g_ref, b_ref, h_ref, hp_ref, tile_scr = refs
    tm = x_ref.shape[0]
    mix = (jnp.dot(orw_ref[...].astype(BF16), w_ref[0:D_RWKV, :], preferred_element_type=F32)
           + jnp.dot(ocv_ref[...].astype(BF16), w_ref[D_RWKV:, :], preferred_element_type=F32))
    h = _layer_norm(ALPHA * x_ref[...] + mix, g_ref[...], b_ref[...])
    h_ref[...] = h
    for s in range(SLABS):
        tile_scr[pl.ds(s, tm, stride=SLABS), :] = h[:, s * LANES:(s + 1) * LANES]
    hp_ref[...] = tile_scr[...].astype(BF16)


def _outproj_ln(o_rwkv, o_conv, x, w_bf, g, b, tm, n_total, row_off, prev=None):
    m = x.shape[0]
    blk_off = row_off // tm
    vec = pl.BlockSpec((1, D_MODEL), lambda i: (0, 0))
    in_specs = [pl.BlockSpec((tm, D_RWKV), lambda i: (i, 0)),
                pl.BlockSpec((tm, D_CONV), lambda i: (i, 0)),
                pl.BlockSpec((tm, D_MODEL), lambda i: (i, 0)),
                pl.BlockSpec((D_MODEL, D_MODEL), lambda i: (0, 0)), vec, vec]
    args = [o_rwkv, o_conv, x, w_bf, g, b]
    aliases = {}
    if prev is not None:
        in_specs += [pl.BlockSpec(memory_space=pl.ANY)] * 2
        args += list(prev)
        aliases = {6: 0, 7: 1}
    return pl.pallas_call(
        functools.partial(_outproj_kernel, aliased=prev is not None), grid=(m // tm,),
        in_specs=in_specs,
        out_specs=[pl.BlockSpec((tm, D_MODEL), lambda i: (blk_off + i, 0)),
                   pl.BlockSpec((tm * SLABS, LANES), lambda i: (blk_off + i, 0))],
        out_shape=[jax.ShapeDtypeStruct((n_total, D_MODEL), F32),
                   jax.ShapeDtypeStruct((n_total * SLABS, LANES), BF16)],
        scratch_shapes=[pltpu.VMEM((tm * SLABS, LANES), F32)],
        input_output_aliases=aliases,
        compiler_params=_cparams(("arbitrary",)), name="outproj_ln")(*args)


def _router_kernel(h_ref, wr_ref, bias_ref, idx_ref, gate_ref, rank_ref, cnt_ref, run_scr):
    i = pl.program_id(0)
    tt = h_ref.shape[0]

    @pl.when(i == 0)
    def _():
        run_scr[...] = jnp.zeros_like(run_scr)

    neg = -jnp.inf
    logits = lax.dot_general(wr_ref[...], h_ref[...].astype(BF16), (((1,), (1,)), ((), ())),
                             preferred_element_type=F32)
    s = _sigmoid(logits)
    biased = s + bias_ref[...]
    g3 = biased.reshape(N_GROUPS, GROUP_SIZE, tt)
    io_g = lax.broadcasted_iota(I32, (N_GROUPS, GROUP_SIZE, tt), 1).astype(F32)
    m1 = jnp.max(g3, axis=1, keepdims=True)
    f1 = jnp.min(jnp.where(g3 == m1, io_g, float(GROUP_SIZE)), axis=1, keepdims=True)
    m2 = jnp.max(jnp.where(io_g == f1, neg, g3), axis=1, keepdims=True)
    score = (m1 + m2).reshape(N_GROUPS, tt)
    io8 = lax.broadcasted_iota(I32, (N_GROUPS, tt), 0).astype(F32)
    gsel = jnp.zeros((N_GROUPS, tt), F32)
    for _ in range(TOPK_GROUPS):
        m = jnp.max(score, axis=0, keepdims=True)
        f = jnp.min(jnp.where(score == m, io8, float(N_GROUPS)), axis=0, keepdims=True)
        hit = io8 == f
        gsel = jnp.where(hit, 1.0, gsel)
        score = jnp.where(hit, neg, score)
    emask = jnp.broadcast_to(gsel.reshape(N_GROUPS, 1, tt), (N_GROUPS, GROUP_SIZE, tt)).reshape(N_EXPERTS, tt)
    masked = jnp.where(emask > 0.0, biased, neg)
    io_e = lax.broadcasted_iota(I32, (N_EXPERTS, tt), 0).astype(F32)
    hits, idxs, sels = [], [], []
    for _ in range(TOP_K):
        m = jnp.max(masked, axis=0, keepdims=True)
        f = jnp.min(jnp.where(masked == m, io_e, float(N_EXPERTS)), axis=0, keepdims=True)
        hit = io_e == f
        hits.append(hit)
        idxs.append(f)
        sels.append(jnp.sum(jnp.where(hit, s, 0.0), axis=0, keepdims=True))
        masked = jnp.where(hit, neg, masked)
    denom = sels[0]
    for k in range(1, TOP_K):
        denom = denom + sels[k]
    chosen = jnp.zeros((N_EXPERTS, tt), F32)
    for hit in hits:
        chosen = jnp.where(hit, 1.0, chosen)
    ur = lax.broadcasted_iota(I32, (tt, tt), 0)
    uc = lax.broadcasted_iota(I32, (tt, tt), 1)
    upper = (ur < uc).astype(BF16)
    before = run_scr[...] + jnp.dot(chosen.astype(BF16), upper, preferred_element_type=F32)
    ranks = [jnp.sum(jnp.where(hit, before, 0.0), axis=0, keepdims=True) for hit in hits]
    idx_ref[...] = jnp.concatenate(idxs, axis=0).astype(I32)
    gate_ref[...] = jnp.concatenate([sk / denom * ROUTED_SCALE for sk in sels], axis=0)
    rank_ref[...] = jnp.concatenate(ranks, axis=0).astype(I32)
    run = run_scr[...] + jnp.sum(chosen, axis=1, keepdims=True)
    run_scr[...] = run
    cnt_ref[...] = jnp.broadcast_to(run, (N_EXPERTS, LANES)).astype(I32)


def _router(h_all, wr_t_bf, bias_col, tt):
    n = h_all.shape[0]
    tokrow = pl.BlockSpec((TOP_K, tt), lambda i: (0, i))
    return pl.pallas_call(
        _router_kernel, grid=(n // tt,),
        in_specs=[pl.BlockSpec((tt, D_MODEL), lambda i: (i, 0)),
                  pl.BlockSpec((N_EXPERTS, D_MODEL), lambda i: (0, 0)),
                  pl.BlockSpec((N_EXPERTS, 1), lambda i: (0, 0))],
        out_specs=[tokrow, tokrow, tokrow, pl.BlockSpec((N_EXPERTS, LANES), lambda i: (0, 0))],
        out_shape=[jax.ShapeDtypeStruct((TOP_K, n), I32), jax.ShapeDtypeStruct((TOP_K, n), F32),
                   jax.ShapeDtypeStruct((TOP_K, n), I32), jax.ShapeDtypeStruct((N_EXPERTS, LANES), I32)],
        scratch_shapes=[pltpu.VMEM((N_EXPERTS, 1), F32)],
        compiler_params=_cparams(("arbitrary",)), name="router")(h_all, wr_t_bf, bias_col)


def _dispatch_kernel(pstart_ref, idx_ref, rank_ref, hp_ref, xs_ref, sem):
    tt = idx_ref.shape[1]

    def row_copy(src_row, dst_row):
        return pltpu.make_async_copy(hp_ref.at[pl.ds(pl.multiple_of(src_row * SLABS, SLABS), SLABS), :],
                                     xs_ref.at[pl.ds(pl.multiple_of(dst_row * SLABS, SLABS), SLABS), :], sem)

    def issue(t, carry):
        for k in range(TOP_K):
            row_copy(t, pstart_ref[idx_ref[k, t]] + rank_ref[k, t]).start(priority=k % 2)
        return carry

    def drain(t, carry):
        for k in range(TOP_K):
            row_copy(0, 0).wait()
        return carry

    lax.fori_loop(0, tt, issue, 0)
    lax.fori_loop(0, tt, drain, 0)


def _dispatch(pstart, idx_t, rank_t, hp, n_rows_padded, tt):
    n = idx_t.shape[1]
    smem_tok = pl.BlockSpec((TOP_K, tt), lambda i, ps: (0, i), memory_space=pltpu.SMEM)
    grid_spec = pltpu.PrefetchScalarGridSpec(
        num_scalar_prefetch=1, grid=(n // tt,),
        in_specs=[smem_tok, smem_tok, pl.BlockSpec((tt * SLABS, LANES), lambda i, ps: (i, 0))],
        out_specs=pl.BlockSpec(memory_space=pl.ANY),
        scratch_shapes=[pltpu.SemaphoreType.DMA])
    return pl.pallas_call(
        _dispatch_kernel, grid_spec=grid_spec,
        out_shape=jax.ShapeDtypeStruct((n_rows_padded * SLABS, LANES), BF16),
        compiler_params=_cparams(("arbitrary",)), name="dispatch")(pstart, idx_t, rank_t, hp)


def _experts_kernel(be_ref, nu_ref, pend_ref, xs_ref, wg_hbm, wu_hbm, wd_hbm, ys_ref,
                    wg_f, wu_f, wd_f, wg_s, wu_s, wd_s, xf_scr, sems, ord_ref):
    b = pl.program_id(0)
    nu = nu_ref[0]
    last = pl.num_programs(0) - 1
    e = be_ref[b]
    live = b < nu
    fresh = (b == 0) | (e != be_ref[jnp.maximum(b - 1, 0)])
    mats = ((wg_hbm, wg_f, wg_s), (wu_hbm, wu_f, wu_s), (wd_hbm, wd_f, wd_s))

    def fetch(j, expert, slot):
        return pltpu.make_async_copy(mats[j][0].at[expert], mats[j][1].at[slot], sems.at[slot, j])

    nxt1 = pend_ref[e] // ROW_BLK
    e1 = be_ref[jnp.minimum(nxt1, last)]
    nxt2 = jnp.where(nxt1 < nu, pend_ref[e1] // ROW_BLK, nu)
    e2 = be_ref[jnp.minimum(nxt2, last)]

    @pl.when(b == 0)
    def _():
        ord_ref[0] = 0
        for j in range(3):
            fetch(j, e, 0).start(priority=WEIGHT_DMA_PRIORITY)

        @pl.when(nxt1 < nu)
        def _():
            for j in range(3):
                fetch(j, e1, 1).start(priority=WEIGHT_DMA_PRIORITY)

    @pl.when(live & fresh)
    def _():
        slot = ord_ref[0] % 2
        for j in range(3):
            fetch(j, e, slot).wait()
            mats[j][2][...] = mats[j][1][slot].astype(BF16)

            @pl.when(nxt2 < nu)
            def _():
                fetch(j, e2, slot).start(priority=WEIGHT_DMA_PRIORITY)

        ord_ref[0] = ord_ref[0] + 1

    @pl.when(live)
    def _():
        xf_scr[...] = xs_ref[...].astype(F32)
        x = jnp.concatenate([xf_scr[pl.ds(s, ROW_BLK, stride=SLABS), :].astype(BF16) for s in range(SLABS)],
                            axis=1)
        nt = (((1,), (1,)), ((), ()))
        g = lax.dot_general(x, wg_s[...], nt, preferred_element_type=F32)
        u = lax.dot_general(x, wu_s[...], nt, preferred_element_type=F32)
        hm = (g * _sigmoid(g)) * u
        y = jnp.dot(hm.astype(BF16), wd_s[...], preferred_element_type=F32)
        for s in range(SLABS):
            ys_ref[pl.ds(s, ROW_BLK, stride=SLABS), :] = y[:, s * LANES:(s + 1) * LANES]


def _experts(blk_expert, n_used, pend, xs, wg_t, wu_t, wd, n_blocks):
    row_blk = lambda b, be, nu, pe: (jnp.minimum(b, nu[0] - 1), 0)
    wshape = (D_EXPERT, D_MODEL)
    grid_spec = pltpu.PrefetchScalarGridSpec(
        num_scalar_prefetch=3, grid=(n_blocks,),
        in_specs=[pl.BlockSpec((ROW_BLK * SLABS, LANES), row_blk)] + [pl.BlockSpec(memory_space=pl.ANY)] * 3,
        out_specs=pl.BlockSpec((ROW_BLK * SLABS, LANES), row_blk),
        scratch_shapes=[pltpu.VMEM((2,) + wshape, F32)] * 3 + [pltpu.VMEM(wshape, BF16)] * 3
        + [pltpu.VMEM((ROW_BLK * SLABS, LANES), F32), pltpu.SemaphoreType.DMA((2, 3)), pltpu.SMEM((1,), I32)])
    return pl.pallas_call(
        _experts_kernel, grid_spec=grid_spec,
        out_shape=jax.ShapeDtypeStruct((n_blocks * ROW_BLK * SLABS, LANES), F32),
        compiler_params=_cparams(("arbitrary",), vmem_mb=56), name="experts")(
            blk_expert, n_used, pend, xs, wg_t, wu_t, wd)


def _combine_kernel(pstart_ref, idx_ref, rank_ref, h_ref, gate_ref, swg_ref, swu_ref, swd_ref, g_ref, b_ref,
                    ys_ref, yp_ref, ysm_ref, gbuf, sem, *, n_prompt_tiles):
    i = pl.program_id(0)
    tt = h_ref.shape[0]

    def row_copy(src_row, dst_row):
        return pltpu.make_async_copy(
            ys_ref.at[pl.ds(pl.multiple_of(src_row * SLABS, SLABS), SLABS), :],
            gbuf.at[pl.ds(pl.multiple_of(dst_row * SLABS, SLABS), SLABS), :], sem)

    def issue(t, carry):
        for k in range(TOP_K):
            row_copy(pstart_ref[idx_ref[k, t]] + rank_ref[k, t], k * tt + t).start(priority=k % 2)
        return carry

    def drain(t, carry):
        for k in range(TOP_K):
            row_copy(0, 0).wait()
        return carry

    lax.fori_loop(0, tt, issue, 0)
    h = h_ref[...]
    x = h.astype(BF16)
    nt = (((1,), (1,)), ((), ()))
    gt = lax.dot_general(x, swg_ref[...], nt, preferred_element_type=F32)
    up = lax.dot_general(x, swu_ref[...], nt, preferred_element_type=F32)
    shared = jnp.dot(((gt * _sigmoid(gt)) * up).astype(BF16), swd_ref[...], preferred_element_type=F32)
    lax.fori_loop(0, tt, drain, 0)
    gates = gate_ref[...]
    cols = []
    for s in range(SLABS):
        acc = jnp.zeros((tt, LANES), F32)
        for k in range(TOP_K):
            acc = acc + gates[:, k:k + 1] * gbuf[pl.ds(k * tt * SLABS + s, tt, stride=SLABS), :]
        cols.append(acc)
    routed = jnp.concatenate(cols, axis=1)
    y = _layer_norm(ALPHA * h + (routed + shared), g_ref[...], b_ref[...])

    @pl.when(i < n_prompt_tiles)
    def _():
        yp_ref[...] = y

    @pl.when(i >= n_prompt_tiles)
    def _():
        ysm_ref[...] = y


def _combine(pstart, idx_t, rank_t, h_all, gates_tok, swg_t, swu_t, swd, g, b, ys, n_prompt, tt):
    n = h_all.shape[0]
    npt = n_prompt // tt
    vec = pl.BlockSpec((1, D_MODEL), lambda i, ps: (0, 0))
    full = lambda a: pl.BlockSpec(a.shape, lambda i, ps: (0, 0))
    smem_tok = pl.BlockSpec((TOP_K, tt), lambda i, ps: (0, i), memory_space=pltpu.SMEM)
    grid_spec = pltpu.PrefetchScalarGridSpec(
        num_scalar_prefetch=1, grid=(n // tt,),
        in_specs=[smem_tok, smem_tok,
                  pl.BlockSpec((tt, D_MODEL), lambda i, ps: (i, 0)),
                  pl.BlockSpec((tt, TOP_K), lambda i, ps: (i, 0)),
                  full(swg_t), full(swu_t), full(swd), vec, vec,
                  pl.BlockSpec(memory_space=pl.ANY)],
        out_specs=[pl.BlockSpec((tt, D_MODEL), lambda i, ps: (jnp.minimum(i, npt - 1), 0)),
                   pl.BlockSpec((tt, D_MODEL), lambda i, ps: (jnp.maximum(i - npt, 0), 0))],
        scratch_shapes=[pltpu.VMEM((TOP_K * tt * SLABS, LANES), F32), pltpu.SemaphoreType.DMA])
    return pl.pallas_call(
        functools.partial(_combine_kernel, n_prompt_tiles=npt), grid_spec=grid_spec,
        out_shape=[jax.ShapeDtypeStruct((n_prompt, D_MODEL), F32),
                   jax.ShapeDtypeStruct((n - n_prompt, D_MODEL), F32)],
        compiler_params=_cparams(("arbitrary",)), name="combine_shared_ln")(
            pstart, idx_t, rank_t, h_all, gates_tok, swg_t, swu_t, swd, g, b, ys)


def _mixer(x2d, n_seq, seq_len, s0_bd, shift0, conv0, wts, tm, tn_r, tt_prep, chunk, wkv_nb, tt_conv):
    proj_r = _matmul(x2d, wts["w_r"], tm, tn_r, "inproj_rwkv")
    u = _glu_matmul(x2d, wts["w_ca"], wts["w_cb"], tm, "inproj_glu")
    streams = _rwkv_prep(proj_r, shift0, wts["prep_vecs"], wts["prep_mats"], n_seq, seq_len, tt_prep)
    o_rwkv, s_fin = _wkv(streams, wts["gn_g"], wts["gn_b"], s0_bd, n_seq, seq_len, chunk, wkv_nb)
    conv0_pad = jnp.pad(conv0, ((0, 0), (HALO - (CONV_WIDTH - 1), 0), (0, 0)))
    o_conv = _conv_module(u, conv0_pad, wts["conv_w"], wts["conv_b"], wts["conv_ln_g"], wts["conv_ln_b"],
                          n_seq, seq_len, tt_conv)
    new_shift = proj_r.reshape(n_seq, seq_len, D_SHIFT)[:, seq_len - 1:, :]
    full = jnp.concatenate([conv0, u.reshape(n_seq, seq_len, D_CONV)], axis=1)
    new_conv = full[:, full.shape[1] - (CONV_WIDTH - 1):]
    return o_rwkv, o_conv, _from_block_diag(s_fin), new_shift, new_conv


def kernel(x_prompt, x_sample, state_wkv, state_shift, state_conv, w_in, mu_shift, w0, w2, a0, a2, g2, k_k, k_a, r_k, gn_g, gn_b, conv_w, conv_b, conv_ln_g, conv_ln_b, w_out, ln1_g, ln1_b, w_router, router_bias, exp_w_gate, exp_w_up, exp_w_down, sh_w_gate, sh_w_up, sh_w_down, ln2_g, ln2_b):
    bp, tp, _ = x_prompt.shape
    bs, ts, _ = x_sample.shape
    n_p, n_s = bp * tp, bs * ts
    n_all = n_p + n_s
    row = lambda v: v.reshape(1, -1)
    zpad = jnp.zeros((64, D_RWKV), F32)
    wts = {
        "w_r": w_in[0][:, :D_SHIFT].astype(BF16),
        "w_ca": w_in[0][:, D_SHIFT:D_SHIFT + D_CONV].astype(BF16),
        "w_cb": w_in[0][:, D_SHIFT + D_CONV:].astype(BF16),
        "prep_vecs": [row(mu_shift[0]), row(w0[0]), row(a0[0]), row(k_k[0]), row(k_a[0]), row(r_k[0])],
        "prep_mats": [jnp.concatenate([w2[0], zpad], axis=0).astype(BF16),
                      jnp.concatenate([zpad, a2[0]], axis=0).astype(BF16),
                      g2[0].astype(BF16)],
        "gn_g": row(gn_g[0]), "gn_b": row(gn_b[0]),
        "conv_w": conv_w[0], "conv_b": row(conv_b[0]),
        "conv_ln_g": row(conv_ln_g[0]), "conv_ln_b": row(conv_ln_b[0]),
    }
    w_out_bf = w_out[0].astype(BF16)
    g1, b1 = row(ln1_g[0]), row(ln1_b[0])

    xp = x_prompt.reshape(n_p, D_MODEL)
    zero_s = jnp.zeros((bp, N_PAIRS, LANES, LANES), F32)
    zero_shift = jnp.zeros((bp, 1, D_SHIFT), F32)
    zero_conv = jnp.zeros((bp, CONV_WIDTH - 1, D_CONV), F32)
    orw_p, ocv_p, wkv_p, shift_p, conv_p = _mixer(xp, bp, tp, zero_s, zero_shift, zero_conv, wts,
                                                  tm=512, tn_r=D_SHIFT // 2, tt_prep=256, chunk=64, wkv_nb=bp,
                                                  tt_conv=128)
    xsm = x_sample.reshape(n_s, D_MODEL)
    orw_s, ocv_s, wkv_s, shift_s, conv_s = _mixer(xsm, bs, ts, _to_block_diag(state_wkv[0]), state_shift[0],
                                                  state_conv[0], wts, tm=n_s, tn_r=D_SHIFT // 2, tt_prep=ts,
                                                  chunk=ts, wkv_nb=1, tt_conv=ts)
    h_all, hp_all = _outproj_ln(orw_p, ocv_p, xp, w_out_bf, g1, b1, 256, n_all, 0)
    h_all, hp_all = _outproj_ln(orw_s, ocv_s, xsm, w_out_bf, g1, b1, n_s, n_all, n_p, prev=(h_all, hp_all))

    tt = ROW_BLK
    idx_t, gate_t, rank_t, cnt = _router(h_all, w_router[0].T.astype(BF16), router_bias[0].reshape(-1, 1), tt)
    counts = cnt[:, 0]
    padded = (counts + ROW_BLK - 1) // ROW_BLK * ROW_BLK
    pend = jnp.cumsum(padded).astype(I32)
    pstart = pend - padded
    n_blocks = _max_blocks(n_all * TOP_K)
    n_used = (pend[-1:] // ROW_BLK).astype(I32)
    blk_row = jnp.minimum(jnp.arange(n_blocks, dtype=I32) * ROW_BLK, pend[-1] - 1)
    blk_expert = jnp.minimum(jnp.sum((pend[None, :] <= blk_row[:, None]).astype(I32), axis=1), N_EXPERTS - 1)
    xs = _dispatch(pstart, idx_t, rank_t, hp_all, n_blocks * ROW_BLK, tt)
    ys = _experts(blk_expert, n_used, pend, xs, jnp.swapaxes(exp_w_gate[0], 1, 2),
                  jnp.swapaxes(exp_w_up[0], 1, 2), exp_w_down[0], n_blocks)
    y_p, y_s = _combine(pstart, idx_t, rank_t, h_all, gate_t.T, sh_w_gate[0].T.astype(BF16),
                        sh_w_up[0].T.astype(BF16), sh_w_down[0].astype(BF16), row(ln2_g[0]), row(ln2_b[0]),
                        ys, n_p, tt)
    return (y_p.reshape(bp, tp, D_MODEL), y_s.reshape(bs, ts, D_MODEL),
            wkv_p[None], shift_p[None], conv_p[None], wkv_s[None], shift_s[None], conv_s[None])
```

```python
import functools
import math

import jax
import jax.numpy as jnp
from jax import lax
from jax.experimental import pallas as pl
from jax.experimental.pallas import tpu as pltpu

F32 = jnp.float32
BF16 = jnp.bfloat16
I32 = jnp.int32
HIGHEST = lax.Precision.HIGHEST

D_MODEL = 2048
D_RWKV = 1024
D_CONV = 1024
HEAD_DIM = 64
N_HEADS = D_RWKV // HEAD_DIM
N_PAIRS = N_HEADS // 2
CONV_WIDTH = 31
D_SHIFT = 3 * D_RWKV + 64 + 64 + 128
N_EXPERTS = 256
TOP_K = 8
N_GROUPS = 8
GROUP_SIZE = N_EXPERTS // N_GROUPS
TOPK_GROUPS = 4
D_EXPERT = 576
ROUTED_SCALE = 2.5
ALPHA = 2.0 ** 0.25
LN_EPS = 1e-5
GN_EPS = 64e-5

LANES = 128
ROW_BLK = 128
STEP_ROWS = 2 * ROW_BLK
TOK_TILE = 128
SLABS = D_MODEL // LANES
WEIGHT_DMA_PRIORITY = 1


def _max_steps(n_rows):
    return -(-(n_rows + N_EXPERTS * (STEP_ROWS - 1)) // STEP_ROWS)


def _cparams(sem, vmem_mb=48):
    return pltpu.CompilerParams(dimension_semantics=sem, vmem_limit_bytes=vmem_mb * 2 ** 20)


def _sigmoid(x):
    return 1.0 / (1.0 + jnp.exp(-x))


def _dot(a, b):
    return jnp.dot(a.astype(BF16), b.astype(BF16), preferred_element_type=F32)


def _dot_nt(a, b):
    return lax.dot_general(a.astype(BF16), b.astype(BF16), (((1,), (1,)), ((), ())),
                           preferred_element_type=F32)


def _dot_tn(a, b):
    return lax.dot_general(a.astype(BF16), b.astype(BF16), (((0,), (0,)), ((), ())),
                           preferred_element_type=F32)


def _head_ones():
    r = lax.broadcasted_iota(I32, (LANES, LANES), 0) // HEAD_DIM
    c = lax.broadcasted_iota(I32, (LANES, LANES), 1) // HEAD_DIM
    return (r == c).astype(F32)


def _segsum(x, ones_bd):
    outs = [jnp.dot(x[:, s * LANES:(s + 1) * LANES], ones_bd, precision=HIGHEST,
                    preferred_element_type=F32) for s in range(x.shape[1] // LANES)]
    return outs[0] if len(outs) == 1 else jnp.concatenate(outs, axis=1)


def _layer_norm(x, g, b):
    mu = jnp.mean(x, axis=-1, keepdims=True)
    d = x - mu
    var = jnp.mean(d * d, axis=-1, keepdims=True)
    return d * lax.rsqrt(var + LN_EPS) * g + b


def _mm_kernel(x_ref, w_ref, o_ref):
    o_ref[...] = jnp.dot(x_ref[...].astype(BF16), w_ref[...], preferred_element_type=F32)


def _matmul(x, w_bf, tm, tn, name):
    m, k = x.shape
    n = w_bf.shape[1]
    return pl.pallas_call(
        _mm_kernel, grid=(n // tn, m // tm),
        in_specs=[pl.BlockSpec((tm, k), lambda j, i: (i, 0)),
                  pl.BlockSpec((k, tn), lambda j, i: (0, j))],
        out_specs=pl.BlockSpec((tm, tn), lambda j, i: (i, j)),
        out_shape=jax.ShapeDtypeStruct((m, n), F32),
        compiler_params=_cparams(("arbitrary", "arbitrary")), name=name)(x, w_bf)


def _glu_kernel(x_ref, wa_ref, wb_ref, o_ref):
    x = x_ref[...].astype(BF16)
    a = jnp.dot(x, wa_ref[...], preferred_element_type=F32)
    b = jnp.dot(x, wb_ref[...], preferred_element_type=F32)
    o_ref[...] = a * _sigmoid(b)


def _glu_matmul(x, wa_bf, wb_bf, tm, name):
    m, k = x.shape
    n = wa_bf.shape[1]
    return pl.pallas_call(
        _glu_kernel, grid=(m // tm,),
        in_specs=[pl.BlockSpec((tm, k), lambda i: (i, 0)),
                  pl.BlockSpec((k, n), lambda i: (0, 0)),
                  pl.BlockSpec((k, n), lambda i: (0, 0))],
        out_specs=pl.BlockSpec((tm, n), lambda i: (i, 0)),
        out_shape=jax.ShapeDtypeStruct((m, n), F32),
        compiler_params=_cparams(("arbitrary",)), name=name)(x, wa_bf, wb_bf)


def _prep_kernel(p_ref, pv_ref, sh_ref, mu_ref, w0_ref, a0_ref, kk_ref, ka_ref, rk_ref,
                 w2_ref, a2_ref, g2_ref,
                 r_out, lw_out, k_out, v_out, kn_out, b_out, g_out, bonus_out):
    i = pl.program_id(1)
    p = p_ref[...]
    tt = p.shape[0]
    carry = jnp.where(i == 0, sh_ref[...], pv_ref[7:8, :])
    row = lax.broadcasted_iota(I32, (tt, 1), 0)
    prev = jnp.where(row == 0, carry, pltpu.roll(p, 1, 0))
    xs = p + (prev - p) * mu_ref[...]
    r = xs[:, 0:D_RWKV]
    k = xs[:, D_RWKV:2 * D_RWKV]
    v = xs[:, 2 * D_RWKV:3 * D_RWKV]
    wa = xs[:, 3 * D_RWKV:3 * D_RWKV + 128]
    gd = xs[:, 3 * D_RWKV + 128:]
    z = w0_ref[...] + _dot(jnp.tanh(wa), w2_ref[...])
    w_log = -(jnp.maximum(-z, 0.0) + jnp.log1p(jnp.exp(-jnp.abs(z)))) - 0.5
    lw_out[...] = -jnp.exp(w_log)
    a = _sigmoid(a0_ref[...] + _dot(wa, a2_ref[...]))
    g_out[...] = _dot(_sigmoid(gd), g2_ref[...])
    ones_bd = _head_ones()
    kk = k * kk_ref[...]
    nrm = jnp.sqrt(_segsum(kk * kk, ones_bd))
    kn = kk / jnp.maximum(nrm, 1e-12)
    kh = k * (1.0 + (a - 1.0) * ka_ref[...])
    r_out[...] = r
    k_out[...] = kh
    v_out[...] = v
    kn_out[...] = kn
    b_out[...] = kn * a
    bonus_out[...] = _segsum(r * kh * rk_ref[...], ones_bd) * v


def _rwkv_prep(proj, shift0, vecs, mats, n_seq, seq_len, tt):
    nt = seq_len // tt
    n = n_seq * seq_len
    row_vec = lambda c: pl.BlockSpec((1, c), lambda b, i: (0, 0))
    full = lambda a: pl.BlockSpec(a.shape, lambda b, i: (0, 0))
    out_spec = pl.BlockSpec((tt, D_RWKV), lambda b, i: (b * nt + i, 0))
    outs = pl.pallas_call(
        _prep_kernel, grid=(n_seq, nt),
        in_specs=[pl.BlockSpec((tt, D_SHIFT), lambda b, i: (b * nt + i, 0)),
                  pl.BlockSpec((8, D_SHIFT), lambda b, i: (jnp.maximum((b * nt + i) * (tt // 8) - 1, 0), 0)),
                  pl.BlockSpec((None, 1, D_SHIFT), lambda b, i: (b, 0, 0)),
                  row_vec(D_SHIFT)] + [row_vec(D_RWKV)] * 5 + [full(m) for m in mats],
        out_specs=[out_spec] * 8,
        out_shape=[jax.ShapeDtypeStruct((n, D_RWKV), F32)] * 8,
        compiler_params=_cparams(("arbitrary", "arbitrary")), name="rwkv_prep")(
            proj, proj, shift0, *vecs, *mats)
    return outs


def _wkv_kernel(r_ref, lw_ref, k_ref, v_ref, kn_ref, b_ref, g_ref, bonus_ref, gng_ref, gnb_ref, s0_ref,
                o_ref, sfin_ref, s_scr):
    c = pl.program_id(1)
    nc = pl.num_programs(1)
    nb, C = r_ref.shape[0], r_ref.shape[1]
    C2 = 2 * C
    fused = C2 == LANES
    pairs = range(nb * N_PAIRS)

    @pl.when(c == 0)
    def _():
        s_scr[...] = s0_ref[...]

    lane = lax.broadcasted_iota(I32, (C, LANES), 1)
    m0 = lane < HEAD_DIM
    ri = lax.broadcasted_iota(I32, (C2, C2), 0)
    ci = lax.broadcasted_iota(I32, (C2, C2), 1)
    same = (ri // C) == (ci // C)
    strict = same & (ri > ci)
    incl = same & (ri >= ci)
    eye = (ri == ci).astype(F32)
    tr = lax.broadcasted_iota(I32, (C, C), 0)
    tc = lax.broadcasted_iota(I32, (C, C), 1)
    ltri = (tr >= tc).astype(BF16)
    ones_bd = _head_ones().astype(BF16)
    n_sq = int(math.log2(C)) - 1

    def stack(x):
        return jnp.concatenate([jnp.where(m0, x, 0.0), jnp.where(m0, 0.0, x)], axis=0).astype(BF16)

    def split_dot(lhs_bf, x, rhs_bf, terms):
        acc, rem = None, x
        for _ in range(terms):
            piece = rem.astype(BF16)
            part = (jnp.dot(lhs_bf, piece, preferred_element_type=F32) if rhs_bf is None
                    else jnp.dot(piece, rhs_bf, preferred_element_type=F32))
            acc = part if acc is None else acc + part
            rem = rem - piece.astype(F32)
        return acc

    lw_all = [lw_ref[q] for q in range(nb)]
    lc_all = [split_dot(ltri, lw, None, 3) for lw in lw_all]
    sls = [slice((p % N_PAIRS) * LANES, (p % N_PAIRS + 1) * LANES) for p in pairs]
    seq = [p // N_PAIRS for p in pairs]

    ar, bk, vst, endst, decay_end = [], [], [], [], []
    for p in pairs:
        sl, q = sls[p], seq[p]
        lw, lc = lw_all[q][:, sl], lc_all[q][:, sl]
        lc_end = lc[C - 1:C, :]
        e_neg = jnp.exp(-lc)
        e_end = jnp.exp(lc_end - lc)
        bb, kh = b_ref[q, :, sl], k_ref[q, :, sl]
        ar.append(jnp.concatenate([stack(-kn_ref[q, :, sl] * jnp.exp(lc - lw)),
                                   stack(r_ref[q, :, sl] * jnp.exp(lc))], axis=0))
        bk.append(jnp.concatenate([stack(bb * e_neg), stack(kh * e_neg)], axis=0))
        vst.append(stack(v_ref[q, :, sl]))
        endst.append(jnp.concatenate([stack(bb * e_end), stack(kh * e_end)], axis=0))
        decay_end.append(jnp.exp(lc_end))

    ab, ak, rbk = [], [], []
    for p in pairs:
        if fused:
            m = _dot_nt(ar[p], bk[p])
            ab.append(jnp.where(strict, m[:C2, :C2], 0.0))
            ak.append(jnp.where(strict, m[:C2, C2:], 0.0).astype(BF16))
            rbk.append(jnp.concatenate([jnp.where(incl, m[C2:, :C2], 0.0),
                                        jnp.where(incl, m[C2:, C2:], 0.0)], axis=1).astype(BF16))
        else:
            a_st, r_st, b_st, k_st = ar[p][:C2], ar[p][C2:], bk[p][:C2], bk[p][C2:]
            ab.append(jnp.where(strict, _dot_nt(a_st, b_st), 0.0))
            ak.append(jnp.where(strict, _dot_nt(a_st, k_st), 0.0).astype(BF16))
            rbk.append((jnp.where(incl, _dot_nt(r_st, b_st), 0.0).astype(BF16),
                        jnp.where(incl, _dot_nt(r_st, k_st), 0.0).astype(BF16)))

    tm = [eye + ab[p] for p in pairs]
    pw = [_dot(ab[p], ab[p]) for p in pairs]
    for _ in range(n_sq - 1):
        for p in pairs:
            both = _dot(jnp.concatenate([tm[p], pw[p]], axis=0), pw[p])
            tm[p] = tm[p] + both[:C2]
            pw[p] = both[C2:]
    for p in pairs:
        tm[p] = (tm[p] + _dot(tm[p], pw[p])).astype(BF16)

    s_old = [s_scr[seq[p], p % N_PAIRS] for p in pairs]
    ars = [_dot_nt(ar[p], s_old[p]) for p in pairs]
    rhs = [ars[p][:C2] + jnp.dot(ak[p], vst[p], preferred_element_type=F32) for p in pairs]
    uv = [jnp.concatenate([jnp.dot(tm[p], rhs[p].astype(BF16), preferred_element_type=F32).astype(BF16),
                           vst[p]], axis=0) for p in pairs]
    for p in pairs:
        s_scr[seq[p], p % N_PAIRS] = s_old[p] * decay_end[p] + _dot_tn(uv[p], endst[p])
    for p in pairs:
        sl, q = sls[p], seq[p]
        if fused:
            o_st = ars[p][C2:] + jnp.dot(rbk[p], uv[p], preferred_element_type=F32)
        else:
            o_st = (ars[p][C2:] + jnp.dot(rbk[p][0], uv[p][:C2], preferred_element_type=F32)
                    + jnp.dot(rbk[p][1], uv[p][C2:], preferred_element_type=F32))
        o = o_st[:C] + o_st[C:]
        mu = split_dot(None, o, ones_bd, 2) * (1.0 / HEAD_DIM)
        d = o - mu
        var = split_dot(None, d * d, ones_bd, 2) * (1.0 / HEAD_DIM)
        on = d * lax.rsqrt(var + GN_EPS) * gng_ref[:, sl] + gnb_ref[:, sl]
        o_ref[q, :, sl] = (on + bonus_ref[q, :, sl]) * g_ref[q, :, sl]

    @pl.when(c == nc - 1)
    def _():
        sfin_ref[...] = s_scr[...]


def _wkv(streams, gn_g, gn_b, s0_bd, n_seq, seq_len, chunk, nb):
    nc = seq_len // chunk
    tok = pl.BlockSpec((nb, chunk, D_RWKV), lambda b, c: (b, c, 0))
    vec = pl.BlockSpec((1, D_RWKV), lambda b, c: (0, 0))
    st = pl.BlockSpec((nb, N_PAIRS, LANES, LANES), lambda b, c: (b, 0, 0, 0))
    o, s_fin = pl.pallas_call(
        _wkv_kernel, grid=(n_seq // nb, nc),
        in_specs=[tok] * 8 + [vec, vec, st],
        out_specs=[tok, st],
        out_shape=[jax.ShapeDtypeStruct((n_seq, seq_len, D_RWKV), F32),
                   jax.ShapeDtypeStruct((n_seq, N_PAIRS, LANES, LANES), F32)],
        scratch_shapes=[pltpu.VMEM((nb, N_PAIRS, LANES, LANES), F32)],
        compiler_params=_cparams(("arbitrary", "arbitrary")), name="wkv_chunks")(
            *[s.reshape(n_seq, seq_len, D_RWKV) for s in streams], gn_g, gn_b, s0_bd)
    return o.reshape(n_seq * seq_len, D_RWKV), s_fin


def _to_block_diag(s):
    b = s.shape[0]
    s = s.reshape(b, N_PAIRS, 2, HEAD_DIM, HEAD_DIM)
    z = jnp.zeros_like(s[:, :, 0])
    top = jnp.concatenate([s[:, :, 0], z], axis=-1)
    bot = jnp.concatenate([z, s[:, :, 1]], axis=-1)
    return jnp.concatenate([top, bot], axis=-2)


def _from_block_diag(bd):
    b = bd.shape[0]
    h0 = bd[:, :, :HEAD_DIM, :HEAD_DIM]
    h1 = bd[:, :, HEAD_DIM:, HEAD_DIM:]
    return jnp.stack([h0, h1], axis=2).reshape(b, N_HEADS, HEAD_DIM, HEAD_DIM)


HALO = 32


def _conv_kernel(u_ref, halo_ref, cp_ref, cw_ref, cb_ref, lg_ref, lb_ref, o_ref, xbuf, ybuf, wbuf):
    i = pl.program_id(1)
    tt = u_ref.shape[0]
    xbuf[0:HALO, :] = jnp.where(i == 0, cp_ref[...], halo_ref[...])
    xbuf[HALO:HALO + tt, :] = u_ref[...]
    for cs in range(D_CONV // LANES):
        sl = slice(cs * LANES, (cs + 1) * LANES)
        for r in range(8):
            wbuf[r] = xbuf[8 - r:8 - r + tt + 24, sl]
        acc = jnp.zeros((tt, LANES), F32)
        for r in range(8):
            for q in range(4):
                s = 8 * q + r
                if s > CONV_WIDTH - 1:
                    continue
                j = CONV_WIDTH - 1 - s
                acc = acc + cw_ref[j:j + 1, sl] * wbuf[r, 24 - 8 * q:24 - 8 * q + tt, :]
        ybuf[:, sl] = acc + cb_ref[:, sl]
    y = _layer_norm(ybuf[...], lg_ref[...], lb_ref[...])
    o_ref[...] = y * _sigmoid(y)


def _conv_module(u, conv0_pad, cw, cb, lg, lb, n_seq, seq_len, tt):
    nt = seq_len // tt
    n = n_seq * seq_len
    vec = pl.BlockSpec((1, D_CONV), lambda b, i: (0, 0))
    return pl.pallas_call(
        _conv_kernel, grid=(n_seq, nt),
        in_specs=[pl.BlockSpec((tt, D_CONV), lambda b, i: (b * nt + i, 0)),
                  pl.BlockSpec((HALO, D_CONV),
                               lambda b, i: (jnp.maximum((b * seq_len + i * tt) // HALO - 1, 0), 0)),
                  pl.BlockSpec((None, HALO, D_CONV), lambda b, i: (b, 0, 0)),
                  pl.BlockSpec((CONV_WIDTH, D_CONV), lambda b, i: (0, 0)), vec, vec, vec],
        out_specs=pl.BlockSpec((tt, D_CONV), lambda b, i: (b * nt + i, 0)),
        out_shape=jax.ShapeDtypeStruct((n, D_CONV), F32),
        scratch_shapes=[pltpu.VMEM((HALO + tt, D_CONV), F32), pltpu.VMEM((tt, D_CONV), F32),
                        pltpu.VMEM((8, tt + 24, LANES), F32)],
        compiler_params=_cparams(("arbitrary", "arbitrary")), name="conv_module")(
            u, u, conv0_pad, cw, cb, lg, lb)


def _outproj_kernel(*refs, aliased):
    if aliased:
        orw_ref, ocv_ref, x_ref, w_ref, g_ref, b_ref, _, _, h_ref, hp_ref, tile_scr = refs
    else:
        orw_ref, ocv_ref, x_ref, w_ref, g_ref, b_ref, h_ref, hp_ref, tile_scr = refs
    tm = x_ref.shape[0]
    mix = (jnp.dot(orw_ref[...].astype(BF16), w_ref[0:D_RWKV, :], preferred_element_type=F32)
           + jnp.dot(ocv_ref[...].astype(BF16), w_ref[D_RWKV:, :], preferred_element_type=F32))
    h = _layer_norm(ALPHA * x_ref[...] + mix, g_ref[...], b_ref[...])
    h_ref[...] = h
    for s in range(SLABS):
        tile_scr[pl.ds(s, tm, stride=SLABS), :] = h[:, s * LANES:(s + 1) * LANES]
    hp_ref[...] = tile_scr[...].astype(BF16)


def _outproj_ln(o_rwkv, o_conv, x, w_bf, g, b, tm, n_total, row_off, prev=None):
    m = x.shape[0]
    blk_off = row_off // tm
    vec = pl.BlockSpec((1, D_MODEL), lambda i: (0, 0))
    in_specs = [pl.BlockSpec((tm, D_RWKV), lambda i: (i, 0)),
                pl.BlockSpec((tm, D_CONV), lambda i: (i, 0)),
                pl.BlockSpec((tm, D_MODEL), lambda i: (i, 0)),
                pl.BlockSpec((D_MODEL, D_MODEL), lambda i: (0, 0)), vec, vec]
    args = [o_rwkv, o_conv, x, w_bf, g, b]
    aliases = {}
    if prev is not None:
        in_specs += [pl.BlockSpec(memory_space=pl.ANY)] * 2
        args += list(prev)
        aliases = {6: 0, 7: 1}
    return pl.pallas_call(
        functools.partial(_outproj_kernel, aliased=prev is not None), grid=(m // tm,),
        in_specs=in_specs,
        out_specs=[pl.BlockSpec((tm, D_MODEL), lambda i: (blk_off + i, 0)),
                   pl.BlockSpec((tm * SLABS, LANES), lambda i: (blk_off + i, 0))],
        out_shape=[jax.ShapeDtypeStruct((n_total, D_MODEL), F32),
                   jax.ShapeDtypeStruct((n_total * SLABS, LANES), BF16)],
        scratch_shapes=[pltpu.VMEM((tm * SLABS, LANES), F32)],
        input_output_aliases=aliases,
        compiler_params=_cparams(("arbitrary",)), name="outproj_ln")(*args)


def _router_kernel(h_ref, wr_ref, bias_ref, idx_ref, gate_ref, rank_ref, cnt_ref, run_scr):
    i = pl.program_id(0)
    tt = h_ref.shape[0]

    @pl.when(i == 0)
    def _():
        run_scr[...] = jnp.zeros_like(run_scr)

    neg = -jnp.inf
    logits = lax.dot_general(wr_ref[...], h_ref[...].astype(BF16), (((1,), (1,)), ((), ())),
                             preferred_element_type=F32)
    s = _sigmoid(logits)
    biased = s + bias_ref[...]
    g3 = biased.reshape(N_GROUPS, GROUP_SIZE, tt)
    io_g = lax.broadcasted_iota(I32, (N_GROUPS, GROUP_SIZE, tt), 1).astype(F32)
    m1 = jnp.max(g3, axis=1, keepdims=True)
    f1 = jnp.min(jnp.where(g3 == m1, io_g, float(GROUP_SIZE)), axis=1, keepdims=True)
    m2 = jnp.max(jnp.where(io_g == f1, neg, g3), axis=1, keepdims=True)
    score = (m1 + m2).reshape(N_GROUPS, tt)
    io8 = lax.broadcasted_iota(I32, (N_GROUPS, tt), 0).astype(F32)
    gsel = jnp.zeros((N_GROUPS, tt), F32)
    for _ in range(TOPK_GROUPS):
        m = jnp.max(score, axis=0, keepdims=True)
        f = jnp.min(jnp.where(score == m, io8, float(N_GROUPS)), axis=0, keepdims=True)
        hit = io8 == f
        gsel = jnp.where(hit, 1.0, gsel)
        score = jnp.where(hit, neg, score)
    emask = jnp.broadcast_to(gsel.reshape(N_GROUPS, 1, tt), (N_GROUPS, GROUP_SIZE, tt)).reshape(N_EXPERTS, tt)
    masked = jnp.where(emask > 0.0, biased, neg)
    io_e = lax.broadcasted_iota(I32, (N_EXPERTS, tt), 0).astype(F32)
    hits, idxs, sels = [], [], []
    for _ in range(TOP_K):
        m = jnp.max(masked, axis=0, keepdims=True)
        f = jnp.min(jnp.where(masked == m, io_e, float(N_EXPERTS)), axis=0, keepdims=True)
        hit = io_e == f
        hits.append(hit)
        idxs.append(f)
        sels.append(jnp.sum(jnp.where(hit, s, 0.0), axis=0, keepdims=True))
        masked = jnp.where(hit, neg, masked)
    denom = sels[0]
    for k in range(1, TOP_K):
        denom = denom + sels[k]
    chosen = jnp.zeros((N_EXPERTS, tt), F32)
    for hit in hits:
        chosen = jnp.where(hit, 1.0, chosen)
    ur = lax.broadcasted_iota(I32, (tt, tt), 0)
    uc = lax.broadcasted_iota(I32, (tt, tt), 1)
    upper = (ur < uc).astype(BF16)
    before = run_scr[...] + jnp.dot(chosen.astype(BF16), upper, preferred_element_type=F32)
    ranks = [jnp.sum(jnp.where(hit, before, 0.0), axis=0, keepdims=True) for hit in hits]
    idx_ref[...] = jnp.concatenate(idxs, axis=0).astype(I32)
    gate_ref[...] = jnp.concatenate([sk / denom * ROUTED_SCALE for sk in sels], axis=0)
    rank_ref[...] = jnp.concatenate(ranks, axis=0).astype(I32)
    run = run_scr[...] + jnp.sum(chosen, axis=1, keepdims=True)
    run_scr[...] = run
    cnt_ref[...] = jnp.broadcast_to(run, (N_EXPERTS, LANES)).astype(I32)


def _router(h_all, wr_t_bf, bias_col, tt):
    n = h_all.shape[0]
    tokrow = pl.BlockSpec((TOP_K, tt), lambda i: (0, i))
    return pl.pallas_call(
        _router_kernel, grid=(n // tt,),
        in_specs=[pl.BlockSpec((tt, D_MODEL), lambda i: (i, 0)),
                  pl.BlockSpec((N_EXPERTS, D_MODEL), lambda i: (0, 0)),
                  pl.BlockSpec((N_EXPERTS, 1), lambda i: (0, 0))],
        out_specs=[tokrow, tokrow, tokrow, pl.BlockSpec((N_EXPERTS, LANES), lambda i: (0, 0))],
        out_shape=[jax.ShapeDtypeStruct((TOP_K, n), I32), jax.ShapeDtypeStruct((TOP_K, n), F32),
                   jax.ShapeDtypeStruct((TOP_K, n), I32), jax.ShapeDtypeStruct((N_EXPERTS, LANES), I32)],
        scratch_shapes=[pltpu.VMEM((N_EXPERTS, 1), F32)],
        compiler_params=_cparams(("arbitrary",)), name="router")(h_all, wr_t_bf, bias_col)


def _dispatch_kernel(pstart_ref, idx_ref, rank_ref, hp_ref, xs_ref, sem):
    tt = idx_ref.shape[1]

    def row_copy(src_row, dst_row):
        return pltpu.make_async_copy(hp_ref.at[pl.ds(pl.multiple_of(src_row * SLABS, SLABS), SLABS), :],
                                     xs_ref.at[pl.ds(pl.multiple_of(dst_row * SLABS, SLABS), SLABS), :], sem)

    def issue(t, carry):
        for k in range(TOP_K):
            row_copy(t, pstart_ref[idx_ref[k, t]] + rank_ref[k, t]).start(priority=k % 2)
        return carry

    def drain(t, carry):
        for k in range(TOP_K):
            row_copy(0, 0).wait()
        return carry

    lax.fori_loop(0, tt, issue, 0)
    lax.fori_loop(0, tt, drain, 0)


def _dispatch(pstart, idx_t, rank_t, hp, n_rows_padded, tt):
    n = idx_t.shape[1]
    smem_tok = pl.BlockSpec((TOP_K, tt), lambda i, ps: (0, i), memory_space=pltpu.SMEM)
    grid_spec = pltpu.PrefetchScalarGridSpec(
        num_scalar_prefetch=1, grid=(n // tt,),
        in_specs=[smem_tok, smem_tok, pl.BlockSpec((tt * SLABS, LANES), lambda i, ps: (i, 0))],
        out_specs=pl.BlockSpec(memory_space=pl.ANY),
        scratch_shapes=[pltpu.SemaphoreType.DMA])
    return pl.pallas_call(
        _dispatch_kernel, grid_spec=grid_spec,
        out_shape=jax.ShapeDtypeStruct((n_rows_padded * SLABS, LANES), BF16),
        compiler_params=_cparams(("arbitrary",)), name="dispatch")(pstart, idx_t, rank_t, hp)


UP_OFF = -(-D_EXPERT // LANES) * LANES


def _experts_kernel(se_ref, sn_ref, nu_ref, send_ref, xs_ref, wg_hbm, wu_hbm, wd_hbm, ys_ref,
                    wg_f, wu_f, wd_f, wgu_s, wd_s, xf_scr, sems, ord_ref):
    b = pl.program_id(0)
    nu = nu_ref[0]
    last = pl.num_programs(0) - 1
    e = se_ref[b]
    live = b < nu
    fresh = (b == 0) | (e != se_ref[jnp.maximum(b - 1, 0)])
    srcs, bufs = (wg_hbm, wu_hbm, wd_hbm), (wg_f, wu_f, wd_f)

    def fetch(j, expert, slot):
        return pltpu.make_async_copy(srcs[j].at[expert], bufs[j].at[slot], sems.at[slot, j])

    def stage(j, slot):
        if j == 0:
            wgu_s[0:D_EXPERT, :] = wg_f[slot].astype(BF16)
        elif j == 1:
            wgu_s[UP_OFF:UP_OFF + D_EXPERT, :] = wu_f[slot].astype(BF16)
        else:
            wd_s[...] = wd_f[slot].astype(BF16)

    nxt1 = send_ref[e]
    e1 = se_ref[jnp.minimum(nxt1, last)]
    nxt2 = jnp.where(nxt1 < nu, send_ref[e1], nu)
    e2 = se_ref[jnp.minimum(nxt2, last)]

    @pl.when(b == 0)
    def _():
        ord_ref[0] = 0
        wgu_s[D_EXPERT:UP_OFF, :] = jnp.zeros((UP_OFF - D_EXPERT, D_MODEL), BF16)
        for j in range(3):
            fetch(j, e, 0).start(priority=WEIGHT_DMA_PRIORITY)

        @pl.when(nxt1 < nu)
        def _():
            for j in range(3):
                fetch(j, e1, 1).start(priority=WEIGHT_DMA_PRIORITY)

    @pl.when(live & fresh)
    def _():
        slot = ord_ref[0] % 2
        for j in range(3):
            fetch(j, e, slot).wait()
            stage(j, slot)

            @pl.when(nxt2 < nu)
            def _():
                fetch(j, e2, slot).start(priority=WEIGHT_DMA_PRIORITY)

        ord_ref[0] = ord_ref[0] + 1

    blk_rows = ROW_BLK * SLABS

    def ffn(n_blk):
        m = n_blk * ROW_BLK
        x = jnp.concatenate([xf_scr[pl.ds(s, m, stride=SLABS), :].astype(BF16) for s in range(SLABS)],
                            axis=1)
        gu = lax.dot_general(x, wgu_s[...], (((1,), (1,)), ((), ())), preferred_element_type=F32)
        g, u = gu[:, :D_EXPERT], gu[:, UP_OFF:UP_OFF + D_EXPERT]
        hm = (g * _sigmoid(g)) * u
        return jnp.dot(hm.astype(BF16), wd_s[...], preferred_element_type=F32)

    def store(n_blk, y):
        for s in range(SLABS):
            ys_ref[pl.ds(s, n_blk * ROW_BLK, stride=SLABS), :] = y[:, s * LANES:(s + 1) * LANES]

    for n_blk in (1, 2):
        @pl.when(live & (sn_ref[b] == n_blk))
        def _():
            rows = n_blk * blk_rows
            xf_scr[0:rows, :] = xs_ref[0:rows, :].astype(F32)
            store(n_blk, ffn(n_blk))


def _experts(step_expert, step_nblk, n_used, step_end, xs, wg_t, wu_t, wd, n_steps):
    wshape = (D_EXPERT, D_MODEL)
    window = pl.BlockSpec((STEP_ROWS * SLABS, LANES), lambda b, se, sn, nu, sd: (jnp.minimum(b, nu[0] - 1), 0))
    grid_spec = pltpu.PrefetchScalarGridSpec(
        num_scalar_prefetch=4, grid=(n_steps,),
        in_specs=[window] + [pl.BlockSpec(memory_space=pl.ANY)] * 3,
        out_specs=window,
        scratch_shapes=[pltpu.VMEM((2,) + wshape, F32)] * 3
        + [pltpu.VMEM((UP_OFF + D_EXPERT, D_MODEL), BF16), pltpu.VMEM(wshape, BF16),
           pltpu.VMEM((STEP_ROWS * SLABS, LANES), F32), pltpu.SemaphoreType.DMA((2, 3)), pltpu.SMEM((1,), I32)])
    return pl.pallas_call(
        _experts_kernel, grid_spec=grid_spec,
        out_shape=jax.ShapeDtypeStruct((n_steps * STEP_ROWS * SLABS, LANES), F32),
        compiler_params=_cparams(("arbitrary",), vmem_mb=56), name="experts")(
            step_expert, step_nblk, n_used, step_end, xs, wg_t, wu_t, wd)


def _combine_kernel(pstart_ref, idx_ref, rank_ref, h_ref, gate_ref, swg_ref, swu_ref, swd_ref, g_ref, b_ref,
                    ys_ref, yp_ref, ysm_ref, gbuf, sem, *, n_prompt_tiles):
    i = pl.program_id(0)
    tt = h_ref.shape[0]

    def row_copy(src_row, dst_row):
        return pltpu.make_async_copy(
            ys_ref.at[pl.ds(pl.multiple_of(src_row * SLABS, SLABS), SLABS), :],
            gbuf.at[pl.ds(pl.multiple_of(dst_row * SLABS, SLABS), SLABS), :], sem)

    def issue(t, carry):
        for k in range(TOP_K):
            row_copy(pstart_ref[idx_ref[k, t]] + rank_ref[k, t], k * tt + t).start(priority=k % 2)
        return carry

    def drain(t, carry):
        for k in range(TOP_K):
            row_copy(0, 0).wait()
        return carry

    lax.fori_loop(0, tt, issue, 0)
    h = h_ref[...]
    x = h.astype(BF16)
    nt = (((1,), (1,)), ((), ()))
    gt = lax.dot_general(x, swg_ref[...], nt, preferred_element_type=F32)
    up = lax.dot_general(x, swu_ref[...], nt, preferred_element_type=F32)
    shared = jnp.dot(((gt * _sigmoid(gt)) * up).astype(BF16), swd_ref[...], preferred_element_type=F32)
    lax.fori_loop(0, tt, drain, 0)
    gates = gate_ref[...]
    cols = []
    for s in range(SLABS):
        acc = jnp.zeros((tt, LANES), F32)
        for k in range(TOP_K):
            acc = acc + gates[:, k:k + 1] * gbuf[pl.ds(k * tt * SLABS + s, tt, stride=SLABS), :]
        cols.append(acc)
    routed = jnp.concatenate(cols, axis=1)
    y = _layer_norm(ALPHA * h + (routed + shared), g_ref[...], b_ref[...])

    @pl.when(i < n_prompt_tiles)
    def _():
        yp_ref[...] = y

    @pl.when(i >= n_prompt_tiles)
    def _():
        ysm_ref[...] = y


def _combine(pstart, idx_t, rank_t, h_all, gates_tok, swg_t, swu_t, swd, g, b, ys, n_prompt, tt):
    n = h_all.shape[0]
    npt = n_prompt // tt
    vec = pl.BlockSpec((1, D_MODEL), lambda i, ps: (0, 0))
    full = lambda a: pl.BlockSpec(a.shape, lambda i, ps: (0, 0))
    smem_tok = pl.BlockSpec((TOP_K, tt), lambda i, ps: (0, i), memory_space=pltpu.SMEM)
    grid_spec = pltpu.PrefetchScalarGridSpec(
        num_scalar_prefetch=1, grid=(n // tt,),
        in_specs=[smem_tok, smem_tok,
                  pl.BlockSpec((tt, D_MODEL), lambda i, ps: (i, 0)),
                  pl.BlockSpec((tt, TOP_K), lambda i, ps: (i, 0)),
                  full(swg_t), full(swu_t), full(swd), vec, vec,
                  pl.BlockSpec(memory_space=pl.ANY)],
        out_specs=[pl.BlockSpec((tt, D_MODEL), lambda i, ps: (jnp.minimum(i, npt - 1), 0)),
                   pl.BlockSpec((tt, D_MODEL), lambda i, ps: (jnp.maximum(i - npt, 0), 0))],
        scratch_shapes=[pltpu.VMEM((TOP_K * tt * SLABS, LANES), F32), pltpu.SemaphoreType.DMA])
    return pl.pallas_call(
        functools.partial(_combine_kernel, n_prompt_tiles=npt), grid_spec=grid_spec,
        out_shape=[jax.ShapeDtypeStruct((n_prompt, D_MODEL), F32),
                   jax.ShapeDtypeStruct((n - n_prompt, D_MODEL), F32)],
        compiler_params=_cparams(("arbitrary",)), name="combine_shared_ln")(
            pstart, idx_t, rank_t, h_all, gates_tok, swg_t, swu_t, swd, g, b, ys)


def _mixer(x2d, n_seq, seq_len, s0_bd, shift0, conv0, wts, tm, tn_r, tt_prep, chunk, wkv_nb, tt_conv):
    proj_r = _matmul(x2d, wts["w_r"], tm, tn_r, "inproj_rwkv")
    u = _glu_matmul(x2d, wts["w_ca"], wts["w_cb"], tm, "inproj_glu")
    streams = _rwkv_prep(proj_r, shift0, wts["prep_vecs"], wts["prep_mats"], n_seq, seq_len, tt_prep)
    o_rwkv, s_fin = _wkv(streams, wts["gn_g"], wts["gn_b"], s0_bd, n_seq, seq_len, chunk, wkv_nb)
    conv0_pad = jnp.pad(conv0, ((0, 0), (HALO - (CONV_WIDTH - 1), 0), (0, 0)))
    o_conv = _conv_module(u, conv0_pad, wts["conv_w"], wts["conv_b"], wts["conv_ln_g"], wts["conv_ln_b"],
                          n_seq, seq_len, tt_conv)
    new_shift = proj_r.reshape(n_seq, seq_len, D_SHIFT)[:, seq_len - 1:, :]
    full = jnp.concatenate([conv0, u.reshape(n_seq, seq_len, D_CONV)], axis=1)
    new_conv = full[:, full.shape[1] - (CONV_WIDTH - 1):]
    return o_rwkv, o_conv, _from_block_diag(s_fin), new_shift, new_conv


def kernel(x_prompt, x_sample, state_wkv, state_shift, state_conv, w_in, mu_shift, w0, w2, a0, a2, g2, k_k, k_a, r_k, gn_g, gn_b, conv_w, conv_b, conv_ln_g, conv_ln_b, w_out, ln1_g, ln1_b, w_router, router_bias, exp_w_gate, exp_w_up, exp_w_down, sh_w_gate, sh_w_up, sh_w_down, ln2_g, ln2_b):
    bp, tp, _ = x_prompt.shape
    bs, ts, _ = x_sample.shape
    n_p, n_s = bp * tp, bs * ts
    n_all = n_p + n_s
    row = lambda v: v.reshape(1, -1)
    zpad = jnp.zeros((64, D_RWKV), F32)
    wts = {
        "w_r": w_in[0][:, :D_SHIFT].astype(BF16),
        "w_ca": w_in[0][:, D_SHIFT:D_SHIFT + D_CONV].astype(BF16),
        "w_cb": w_in[0][:, D_SHIFT + D_CONV:].astype(BF16),
        "prep_vecs": [row(mu_shift[0]), row(w0[0]), row(a0[0]), row(k_k[0]), row(k_a[0]), row(r_k[0])],
        "prep_mats": [jnp.concatenate([w2[0], zpad], axis=0).astype(BF16),
                      jnp.concatenate([zpad, a2[0]], axis=0).astype(BF16),
                      g2[0].astype(BF16)],
        "gn_g": row(gn_g[0]), "gn_b": row(gn_b[0]),
        "conv_w": conv_w[0], "conv_b": row(conv_b[0]),
        "conv_ln_g": row(conv_ln_g[0]), "conv_ln_b": row(conv_ln_b[0]),
    }
    w_out_bf = w_out[0].astype(BF16)
    g1, b1 = row(ln1_g[0]), row(ln1_b[0])

    xp = x_prompt.reshape(n_p, D_MODEL)
    zero_s = jnp.zeros((bp, N_PAIRS, LANES, LANES), F32)
    zero_shift = jnp.zeros((bp, 1, D_SHIFT), F32)
    zero_conv = jnp.zeros((bp, CONV_WIDTH - 1, D_CONV), F32)
    orw_p, ocv_p, wkv_p, shift_p, conv_p = _mixer(xp, bp, tp, zero_s, zero_shift, zero_conv, wts,
                                                  tm=512, tn_r=D_SHIFT // 2, tt_prep=256, chunk=64, wkv_nb=bp,
                                                  tt_conv=128)
    xsm = x_sample.reshape(n_s, D_MODEL)
    orw_s, ocv_s, wkv_s, shift_s, conv_s = _mixer(xsm, bs, ts, _to_block_diag(state_wkv[0]), state_shift[0],
                                                  state_conv[0], wts, tm=n_s, tn_r=D_SHIFT // 2, tt_prep=ts,
                                                  chunk=ts, wkv_nb=1, tt_conv=ts)
    h_all, hp_all = _outproj_ln(orw_p, ocv_p, xp, w_out_bf, g1, b1, 256, n_all, 0)
    h_all, hp_all = _outproj_ln(orw_s, ocv_s, xsm, w_out_bf, g1, b1, n_s, n_all, n_p, prev=(h_all, hp_all))

    tt = TOK_TILE
    idx_t, gate_t, rank_t, cnt = _router(h_all, w_router[0].T.astype(BF16), router_bias[0].reshape(-1, 1), tt)
    counts = cnt[:, 0]
    steps = (counts + STEP_ROWS - 1) // STEP_ROWS
    step_end = jnp.cumsum(steps).astype(I32)
    step_start = step_end - steps
    pstart = step_start * STEP_ROWS
    n_steps = _max_steps(n_all * TOP_K)
    n_used = step_end[-1:]
    step = jnp.minimum(jnp.arange(n_steps, dtype=I32), step_end[-1] - 1)
    step_expert = jnp.minimum(jnp.sum((step_end[None, :] <= step[:, None]).astype(I32), axis=1), N_EXPERTS - 1)
    mine = step_expert[:, None] == jnp.arange(N_EXPERTS, dtype=I32)[None, :]
    rows_left = jnp.sum(jnp.where(mine, counts[None, :] - (step[:, None] - step_start[None, :]) * STEP_ROWS, 0),
                        axis=1)
    step_nblk = jnp.where(rows_left > ROW_BLK, 2, 1).astype(I32)
    xs = _dispatch(pstart, idx_t, rank_t, hp_all, n_steps * STEP_ROWS, tt)
    ys = _experts(step_expert, step_nblk, n_used, step_end, xs, jnp.swapaxes(exp_w_gate[0], 1, 2),
                  jnp.swapaxes(exp_w_up[0], 1, 2), exp_w_down[0], n_steps)
    y_p, y_s = _combine(pstart, idx_t, rank_t, h_all, gate_t.T, sh_w_gate[0].T.astype(BF16),
                        sh_w_up[0].T.astype(BF16), sh_w_down[0].astype(BF16), row(ln2_g[0]), row(ln2_b[0]),
                        ys, n_p, tt)
    return (y_p.reshape(bp, tp, D_MODEL), y_s.reshape(bs, ts, D_MODEL),
            wkv_p[None], shift_p[None], conv_p[None], wkv_s[None], shift_s[None], conv_s[None])
```

```python
import functools
import math

import jax
import jax.numpy as jnp
from jax import lax
from jax.experimental import pallas as pl
from jax.experimental.pallas import tpu as pltpu

F32 = jnp.float32
BF16 = jnp.bfloat16
I32 = jnp.int32
HIGHEST = lax.Precision.HIGHEST

D_MODEL = 2048
D_RWKV = 1024
D_CONV = 1024
HEAD_DIM = 64
N_HEADS = D_RWKV // HEAD_DIM
N_PAIRS = N_HEADS // 2
CONV_WIDTH = 31
D_SHIFT = 3 * D_RWKV + 64 + 64 + 128
N_EXPERTS = 256
TOP_K = 8
N_GROUPS = 8
GROUP_SIZE = N_EXPERTS // N_GROUPS
TOPK_GROUPS = 4
D_EXPERT = 576
ROUTED_SCALE = 2.5
ALPHA = 2.0 ** 0.25
LN_EPS = 1e-5
GN_EPS = 64e-5

LANES = 128
ROW_BLK = 128
STEP_BLKS = 3
STEP_ROWS = STEP_BLKS * ROW_BLK
TOK_TILE = 128
SLABS = D_MODEL // LANES
WEIGHT_DMA_PRIORITY = 1


def _max_steps(n_rows):
    return -(-(n_rows + N_EXPERTS * (STEP_ROWS - 1)) // STEP_ROWS)


def _cparams(sem, vmem_mb=48):
    return pltpu.CompilerParams(dimension_semantics=sem, vmem_limit_bytes=vmem_mb * 2 ** 20)


def _sigmoid(x):
    return 1.0 / (1.0 + jnp.exp(-x))


def _dot(a, b):
    return jnp.dot(a.astype(BF16), b.astype(BF16), preferred_element_type=F32)


def _dot_nt(a, b):
    return lax.dot_general(a.astype(BF16), b.astype(BF16), (((1,), (1,)), ((), ())),
                           preferred_element_type=F32)


def _dot_tn(a, b):
    return lax.dot_general(a.astype(BF16), b.astype(BF16), (((0,), (0,)), ((), ())),
                           preferred_element_type=F32)


def _head_ones():
    r = lax.broadcasted_iota(I32, (LANES, LANES), 0) // HEAD_DIM
    c = lax.broadcasted_iota(I32, (LANES, LANES), 1) // HEAD_DIM
    return (r == c).astype(F32)


def _segsum(x, ones_bd):
    outs = [jnp.dot(x[:, s * LANES:(s + 1) * LANES], ones_bd, precision=HIGHEST,
                    preferred_element_type=F32) for s in range(x.shape[1] // LANES)]
    return outs[0] if len(outs) == 1 else jnp.concatenate(outs, axis=1)


def _layer_norm(x, g, b):
    mu = jnp.mean(x, axis=-1, keepdims=True)
    d = x - mu
    var = jnp.mean(d * d, axis=-1, keepdims=True)
    return d * lax.rsqrt(var + LN_EPS) * g + b


def _mm_kernel(x_ref, w_ref, o_ref):
    o_ref[...] = jnp.dot(x_ref[...].astype(BF16), w_ref[...], preferred_element_type=F32)


def _matmul(x, w_bf, tm, tn, name):
    m, k = x.shape
    n = w_bf.shape[1]
    return pl.pallas_call(
        _mm_kernel, grid=(n // tn, m // tm),
        in_specs=[pl.BlockSpec((tm, k), lambda j, i: (i, 0)),
                  pl.BlockSpec((k, tn), lambda j, i: (0, j))],
        out_specs=pl.BlockSpec((tm, tn), lambda j, i: (i, j)),
        out_shape=jax.ShapeDtypeStruct((m, n), F32),
        compiler_params=_cparams(("arbitrary", "arbitrary")), name=name)(x, w_bf)


def _glu_kernel(x_ref, wa_ref, wb_ref, o_ref):
    x = x_ref[...].astype(BF16)
    a = jnp.dot(x, wa_ref[...], preferred_element_type=F32)
    b = jnp.dot(x, wb_ref[...], preferred_element_type=F32)
    o_ref[...] = a * _sigmoid(b)


def _glu_matmul(x, wa_bf, wb_bf, tm, name):
    m, k = x.shape
    n = wa_bf.shape[1]
    return pl.pallas_call(
        _glu_kernel, grid=(m // tm,),
        in_specs=[pl.BlockSpec((tm, k), lambda i: (i, 0)),
                  pl.BlockSpec((k, n), lambda i: (0, 0)),
                  pl.BlockSpec((k, n), lambda i: (0, 0))],
        out_specs=pl.BlockSpec((tm, n), lambda i: (i, 0)),
        out_shape=jax.ShapeDtypeStruct((m, n), F32),
        compiler_params=_cparams(("arbitrary",)), name=name)(x, wa_bf, wb_bf)


def _prep_kernel(p_ref, pv_ref, sh_ref, mu_ref, w0_ref, a0_ref, kk_ref, ka_ref, rk_ref,
                 w2_ref, a2_ref, g2_ref,
                 r_out, lw_out, k_out, v_out, kn_out, b_out, g_out, bonus_out):
    i = pl.program_id(1)
    p = p_ref[...]
    tt = p.shape[0]
    carry = jnp.where(i == 0, sh_ref[...], pv_ref[7:8, :])
    row = lax.broadcasted_iota(I32, (tt, 1), 0)
    prev = jnp.where(row == 0, carry, pltpu.roll(p, 1, 0))
    xs = p + (prev - p) * mu_ref[...]
    r = xs[:, 0:D_RWKV]
    k = xs[:, D_RWKV:2 * D_RWKV]
    v = xs[:, 2 * D_RWKV:3 * D_RWKV]
    wa = xs[:, 3 * D_RWKV:3 * D_RWKV + 128]
    gd = xs[:, 3 * D_RWKV + 128:]
    z = w0_ref[...] + _dot(jnp.tanh(wa), w2_ref[...])
    w_log = -(jnp.maximum(-z, 0.0) + jnp.log1p(jnp.exp(-jnp.abs(z)))) - 0.5
    lw_out[...] = -jnp.exp(w_log)
    a = _sigmoid(a0_ref[...] + _dot(wa, a2_ref[...]))
    g_out[...] = _dot(_sigmoid(gd), g2_ref[...])
    ones_bd = _head_ones()
    kk = k * kk_ref[...]
    nrm = jnp.sqrt(_segsum(kk * kk, ones_bd))
    kn = kk / jnp.maximum(nrm, 1e-12)
    kh = k * (1.0 + (a - 1.0) * ka_ref[...])
    r_out[...] = r
    k_out[...] = kh
    v_out[...] = v
    kn_out[...] = kn
    b_out[...] = kn * a
    bonus_out[...] = _segsum(r * kh * rk_ref[...], ones_bd) * v


def _rwkv_prep(proj, shift0, vecs, mats, n_seq, seq_len, tt):
    nt = seq_len // tt
    n = n_seq * seq_len
    row_vec = lambda c: pl.BlockSpec((1, c), lambda b, i: (0, 0))
    full = lambda a: pl.BlockSpec(a.shape, lambda b, i: (0, 0))
    out_spec = pl.BlockSpec((tt, D_RWKV), lambda b, i: (b * nt + i, 0))
    outs = pl.pallas_call(
        _prep_kernel, grid=(n_seq, nt),
        in_specs=[pl.BlockSpec((tt, D_SHIFT), lambda b, i: (b * nt + i, 0)),
                  pl.BlockSpec((8, D_SHIFT), lambda b, i: (jnp.maximum((b * nt + i) * (tt // 8) - 1, 0), 0)),
                  pl.BlockSpec((None, 1, D_SHIFT), lambda b, i: (b, 0, 0)),
                  row_vec(D_SHIFT)] + [row_vec(D_RWKV)] * 5 + [full(m) for m in mats],
        out_specs=[out_spec] * 8,
        out_shape=[jax.ShapeDtypeStruct((n, D_RWKV), F32)] * 8,
        compiler_params=_cparams(("arbitrary", "arbitrary")), name="rwkv_prep")(
            proj, proj, shift0, *vecs, *mats)
    return outs


def _wkv_kernel(r_ref, lw_ref, k_ref, v_ref, kn_ref, b_ref, g_ref, bonus_ref, gng_ref, gnb_ref, s0_ref,
                o_ref, sfin_ref, s_scr):
    c = pl.program_id(1)
    nc = pl.num_programs(1)
    nb, C = r_ref.shape[0], r_ref.shape[1]
    C2 = 2 * C
    fused = C2 == LANES
    pairs = range(nb * N_PAIRS)

    @pl.when(c == 0)
    def _():
        s_scr[...] = s0_ref[...]

    lane = lax.broadcasted_iota(I32, (C, LANES), 1)
    m0 = lane < HEAD_DIM
    ri = lax.broadcasted_iota(I32, (C2, C2), 0)
    ci = lax.broadcasted_iota(I32, (C2, C2), 1)
    same = (ri // C) == (ci // C)
    strict = same & (ri > ci)
    incl = same & (ri >= ci)
    eye = (ri == ci).astype(F32)
    tr = lax.broadcasted_iota(I32, (C, C), 0)
    tc = lax.broadcasted_iota(I32, (C, C), 1)
    ltri = (tr >= tc).astype(BF16)
    ones_bd = _head_ones().astype(BF16)
    n_sq = int(math.log2(C)) - 1

    def stack(x):
        return jnp.concatenate([jnp.where(m0, x, 0.0), jnp.where(m0, 0.0, x)], axis=0).astype(BF16)

    def split_dot(lhs_bf, x, rhs_bf, terms):
        acc, rem = None, x
        for _ in range(terms):
            piece = rem.astype(BF16)
            part = (jnp.dot(lhs_bf, piece, preferred_element_type=F32) if rhs_bf is None
                    else jnp.dot(piece, rhs_bf, preferred_element_type=F32))
            acc = part if acc is None else acc + part
            rem = rem - piece.astype(F32)
        return acc

    lw_all = [lw_ref[q] for q in range(nb)]
    lc_all = [split_dot(ltri, lw, None, 3) for lw in lw_all]
    sls = [slice((p % N_PAIRS) * LANES, (p % N_PAIRS + 1) * LANES) for p in pairs]
    seq = [p // N_PAIRS for p in pairs]

    ar, bk, vst, endst, decay_end = [], [], [], [], []
    for p in pairs:
        sl, q = sls[p], seq[p]
        lw, lc = lw_all[q][:, sl], lc_all[q][:, sl]
        lc_end = lc[C - 1:C, :]
        e_neg = jnp.exp(-lc)
        e_end = jnp.exp(lc_end - lc)
        bb, kh = b_ref[q, :, sl], k_ref[q, :, sl]
        ar.append(jnp.concatenate([stack(-kn_ref[q, :, sl] * jnp.exp(lc - lw)),
                                   stack(r_ref[q, :, sl] * jnp.exp(lc))], axis=0))
        bk.append(jnp.concatenate([stack(bb * e_neg), stack(kh * e_neg)], axis=0))
        vst.append(stack(v_ref[q, :, sl]))
        endst.append(jnp.concatenate([stack(bb * e_end), stack(kh * e_end)], axis=0))
        decay_end.append(jnp.exp(lc_end))

    ab, ak, rbk = [], [], []
    for p in pairs:
        if fused:
            m = _dot_nt(ar[p], bk[p])
            ab.append(jnp.where(strict, m[:C2, :C2], 0.0))
            ak.append(jnp.where(strict, m[:C2, C2:], 0.0).astype(BF16))
            rbk.append(jnp.concatenate([jnp.where(incl, m[C2:, :C2], 0.0),
                                        jnp.where(incl, m[C2:, C2:], 0.0)], axis=1).astype(BF16))
        else:
            a_st, r_st, b_st, k_st = ar[p][:C2], ar[p][C2:], bk[p][:C2], bk[p][C2:]
            ab.append(jnp.where(strict, _dot_nt(a_st, b_st), 0.0))
            ak.append(jnp.where(strict, _dot_nt(a_st, k_st), 0.0).astype(BF16))
            rbk.append((jnp.where(incl, _dot_nt(r_st, b_st), 0.0).astype(BF16),
                        jnp.where(incl, _dot_nt(r_st, k_st), 0.0).astype(BF16)))

    tm = [eye + ab[p] for p in pairs]
    pw = [_dot(ab[p], ab[p]) for p in pairs]
    for _ in range(n_sq - 1):
        for p in pairs:
            both = _dot(jnp.concatenate([tm[p], pw[p]], axis=0), pw[p])
            tm[p] = tm[p] + both[:C2]
            pw[p] = both[C2:]
    for p in pairs:
        tm[p] = (tm[p] + _dot(tm[p], pw[p])).astype(BF16)

    s_old = [s_scr[seq[p], p % N_PAIRS] for p in pairs]
    ars = [_dot_nt(ar[p], s_old[p]) for p in pairs]
    rhs = [ars[p][:C2] + jnp.dot(ak[p], vst[p], preferred_element_type=F32) for p in pairs]
    uv = [jnp.concatenate([jnp.dot(tm[p], rhs[p].astype(BF16), preferred_element_type=F32).astype(BF16),
                           vst[p]], axis=0) for p in pairs]
    for p in pairs:
        s_scr[seq[p], p % N_PAIRS] = s_old[p] * decay_end[p] + _dot_tn(uv[p], endst[p])
    for p in pairs:
        sl, q = sls[p], seq[p]
        if fused:
            o_st = ars[p][C2:] + jnp.dot(rbk[p], uv[p], preferred_element_type=F32)
        else:
            o_st = (ars[p][C2:] + jnp.dot(rbk[p][0], uv[p][:C2], preferred_element_type=F32)
                    + jnp.dot(rbk[p][1], uv[p][C2:], preferred_element_type=F32))
        o = o_st[:C] + o_st[C:]
        mu = split_dot(None, o, ones_bd, 2) * (1.0 / HEAD_DIM)
        d = o - mu
        var = split_dot(None, d * d, ones_bd, 2) * (1.0 / HEAD_DIM)
        on = d * lax.rsqrt(var + GN_EPS) * gng_ref[:, sl] + gnb_ref[:, sl]
        o_ref[q, :, sl] = (on + bonus_ref[q, :, sl]) * g_ref[q, :, sl]

    @pl.when(c == nc - 1)
    def _():
        sfin_ref[...] = s_scr[...]


def _wkv(streams, gn_g, gn_b, s0_bd, n_seq, seq_len, chunk, nb):
    nc = seq_len // chunk
    tok = pl.BlockSpec((nb, chunk, D_RWKV), lambda b, c: (b, c, 0))
    vec = pl.BlockSpec((1, D_RWKV), lambda b, c: (0, 0))
    st = pl.BlockSpec((nb, N_PAIRS, LANES, LANES), lambda b, c: (b, 0, 0, 0))
    o, s_fin = pl.pallas_call(
        _wkv_kernel, grid=(n_seq // nb, nc),
        in_specs=[tok] * 8 + [vec, vec, st],
        out_specs=[tok, st],
        out_shape=[jax.ShapeDtypeStruct((n_seq, seq_len, D_RWKV), F32),
                   jax.ShapeDtypeStruct((n_seq, N_PAIRS, LANES, LANES), F32)],
        scratch_shapes=[pltpu.VMEM((nb, N_PAIRS, LANES, LANES), F32)],
        compiler_params=_cparams(("arbitrary", "arbitrary")), name="wkv_chunks")(
            *[s.reshape(n_seq, seq_len, D_RWKV) for s in streams], gn_g, gn_b, s0_bd)
    return o.reshape(n_seq * seq_len, D_RWKV), s_fin


def _to_block_diag(s):
    b = s.shape[0]
    s = s.reshape(b, N_PAIRS, 2, HEAD_DIM, HEAD_DIM)
    z = jnp.zeros_like(s[:, :, 0])
    top = jnp.concatenate([s[:, :, 0], z], axis=-1)
    bot = jnp.concatenate([z, s[:, :, 1]], axis=-1)
    return jnp.concatenate([top, bot], axis=-2)


def _from_block_diag(bd):
    b = bd.shape[0]
    h0 = bd[:, :, :HEAD_DIM, :HEAD_DIM]
    h1 = bd[:, :, HEAD_DIM:, HEAD_DIM:]
    return jnp.stack([h0, h1], axis=2).reshape(b, N_HEADS, HEAD_DIM, HEAD_DIM)


HALO = 32


def _conv_kernel(u_ref, halo_ref, cp_ref, cw_ref, cb_ref, lg_ref, lb_ref, o_ref, xbuf, ybuf, wbuf):
    i = pl.program_id(1)
    tt = u_ref.shape[0]
    xbuf[0:HALO, :] = jnp.where(i == 0, cp_ref[...], halo_ref[...])
    xbuf[HALO:HALO + tt, :] = u_ref[...]
    for cs in range(D_CONV // LANES):
        sl = slice(cs * LANES, (cs + 1) * LANES)
        for r in range(8):
            wbuf[r] = xbuf[8 - r:8 - r + tt + 24, sl]
        acc = jnp.zeros((tt, LANES), F32)
        for r in range(8):
            for q in range(4):
                s = 8 * q + r
                if s > CONV_WIDTH - 1:
                    continue
                j = CONV_WIDTH - 1 - s
                acc = acc + cw_ref[j:j + 1, sl] * wbuf[r, 24 - 8 * q:24 - 8 * q + tt, :]
        ybuf[:, sl] = acc + cb_ref[:, sl]
    y = _layer_norm(ybuf[...], lg_ref[...], lb_ref[...])
    o_ref[...] = y * _sigmoid(y)


def _conv_module(u, conv0_pad, cw, cb, lg, lb, n_seq, seq_len, tt):
    nt = seq_len // tt
    n = n_seq * seq_len
    vec = pl.BlockSpec((1, D_CONV), lambda b, i: (0, 0))
    return pl.pallas_call(
        _conv_kernel, grid=(n_seq, nt),
        in_specs=[pl.BlockSpec((tt, D_CONV), lambda b, i: (b * nt + i, 0)),
                  pl.BlockSpec((HALO, D_CONV),
                               lambda b, i: (jnp.maximum((b * seq_len + i * tt) // HALO - 1, 0), 0)),
                  pl.BlockSpec((None, HALO, D_CONV), lambda b, i: (b, 0, 0)),
                  pl.BlockSpec((CONV_WIDTH, D_CONV), lambda b, i: (0, 0)), vec, vec, vec],
        out_specs=pl.BlockSpec((tt, D_CONV), lambda b, i: (b * nt + i, 0)),
        out_shape=jax.ShapeDtypeStruct((n, D_CONV), F32),
        scratch_shapes=[pltpu.VMEM((HALO + tt, D_CONV), F32), pltpu.VMEM((tt, D_CONV), F32),
                        pltpu.VMEM((8, tt + 24, LANES), F32)],
        compiler_params=_cparams(("arbitrary", "arbitrary")), name="conv_module")(
            u, u, conv0_pad, cw, cb, lg, lb)


def _outproj_kernel(*refs, aliased):
    if aliased:
        orw_ref, ocv_ref, x_ref, w_ref, g_ref, b_ref, _, _, h_ref, hp_ref, tile_scr = refs
    else:
        orw_ref, ocv_ref, x_ref, w_ref, g_ref, b_ref, h_ref, hp_ref, tile_scr = refs
    tm = x_ref.shape[0]
    mix = (jnp.dot(orw_ref[...].astype(BF16), w_ref[0:D_RWKV, :], preferred_element_type=F32)
           + jnp.dot(ocv_ref[...].astype(BF16), w_ref[D_RWKV:, :], preferred_element_type=F32))
    h = _layer_norm(ALPHA * x_ref[...] + mix, g_ref[...], b_ref[...])
    h_ref[...] = h
    for s in range(SLABS):
        tile_scr[pl.ds(s, tm, stride=SLABS), :] = h[:, s * LANES:(s + 1) * LANES]
    hp_ref[...] = tile_scr[...].astype(BF16)


def _outproj_ln(o_rwkv, o_conv, x, w_bf, g, b, tm, n_total, row_off, prev=None):
    m = x.shape[0]
    blk_off = row_off // tm
    vec = pl.BlockSpec((1, D_MODEL), lambda i: (0, 0))
    in_specs = [pl.BlockSpec((tm, D_RWKV), lambda i: (i, 0)),
                pl.BlockSpec((tm, D_CONV), lambda i: (i, 0)),
                pl.BlockSpec((tm, D_MODEL), lambda i: (i, 0)),
                pl.BlockSpec((D_MODEL, D_MODEL), lambda i: (0, 0)), vec, vec]
    args = [o_rwkv, o_conv, x, w_bf, g, b]
    aliases = {}
    if prev is not None:
        in_specs += [pl.BlockSpec(memory_space=pl.ANY)] * 2
        args += list(prev)
        aliases = {6: 0, 7: 1}
    return pl.pallas_call(
        functools.partial(_outproj_kernel, aliased=prev is not None), grid=(m // tm,),
        in_specs=in_specs,
        out_specs=[pl.BlockSpec((tm, D_MODEL), lambda i: (blk_off + i, 0)),
                   pl.BlockSpec((tm * SLABS, LANES), lambda i: (blk_off + i, 0))],
        out_shape=[jax.ShapeDtypeStruct((n_total, D_MODEL), F32),
                   jax.ShapeDtypeStruct((n_total * SLABS, LANES), BF16)],
        scratch_shapes=[pltpu.VMEM((tm * SLABS, LANES), F32)],
        input_output_aliases=aliases,
        compiler_params=_cparams(("arbitrary",)), name="outproj_ln")(*args)


def _router_kernel(h_ref, wr_ref, bias_ref, idx_ref, gate_ref, rank_ref, cnt_ref, run_scr):
    i = pl.program_id(0)
    tt = h_ref.shape[0]

    @pl.when(i == 0)
    def _():
        run_scr[...] = jnp.zeros_like(run_scr)

    neg = -jnp.inf
    logits = lax.dot_general(wr_ref[...], h_ref[...].astype(BF16), (((1,), (1,)), ((), ())),
                             preferred_element_type=F32)
    s = _sigmoid(logits)
    biased = s + bias_ref[...]
    g3 = biased.reshape(N_GROUPS, GROUP_SIZE, tt)
    io_g = lax.broadcasted_iota(I32, (N_GROUPS, GROUP_SIZE, tt), 1).astype(F32)
    m1 = jnp.max(g3, axis=1, keepdims=True)
    f1 = jnp.min(jnp.where(g3 == m1, io_g, float(GROUP_SIZE)), axis=1, keepdims=True)
    m2 = jnp.max(jnp.where(io_g == f1, neg, g3), axis=1, keepdims=True)
    score = (m1 + m2).reshape(N_GROUPS, tt)
    io8 = lax.broadcasted_iota(I32, (N_GROUPS, tt), 0).astype(F32)
    gsel = jnp.zeros((N_GROUPS, tt), F32)
    for _ in range(TOPK_GROUPS):
        m = jnp.max(score, axis=0, keepdims=True)
        f = jnp.min(jnp.where(score == m, io8, float(N_GROUPS)), axis=0, keepdims=True)
        hit = io8 == f
        gsel = jnp.where(hit, 1.0, gsel)
        score = jnp.where(hit, neg, score)
    emask = jnp.broadcast_to(gsel.reshape(N_GROUPS, 1, tt), (N_GROUPS, GROUP_SIZE, tt)).reshape(N_EXPERTS, tt)
    masked = jnp.where(emask > 0.0, biased, neg)
    io_e = lax.broadcasted_iota(I32, (N_EXPERTS, tt), 0).astype(F32)
    hits, idxs, sels = [], [], []
    for _ in range(TOP_K):
        m = jnp.max(masked, axis=0, keepdims=True)
        f = jnp.min(jnp.where(masked == m, io_e, float(N_EXPERTS)), axis=0, keepdims=True)
        hit = io_e == f
        hits.append(hit)
        idxs.append(f)
        sels.append(jnp.sum(jnp.where(hit, s, 0.0), axis=0, keepdims=True))
        masked = jnp.where(hit, neg, masked)
    denom = sels[0]
    for k in range(1, TOP_K):
        denom = denom + sels[k]
    chosen = jnp.zeros((N_EXPERTS, tt), F32)
    for hit in hits:
        chosen = jnp.where(hit, 1.0, chosen)
    ur = lax.broadcasted_iota(I32, (tt, tt), 0)
    uc = lax.broadcasted_iota(I32, (tt, tt), 1)
    upper = (ur < uc).astype(BF16)
    before = run_scr[...] + jnp.dot(chosen.astype(BF16), upper, preferred_element_type=F32)
    ranks = [jnp.sum(jnp.where(hit, before, 0.0), axis=0, keepdims=True) for hit in hits]
    idx_ref[...] = jnp.concatenate(idxs, axis=0).astype(I32)
    gate_ref[...] = jnp.concatenate([sk / denom * ROUTED_SCALE for sk in sels], axis=0)
    rank_ref[...] = jnp.concatenate(ranks, axis=0).astype(I32)
    run = run_scr[...] + jnp.sum(chosen, axis=1, keepdims=True)
    run_scr[...] = run
    cnt_ref[...] = jnp.broadcast_to(run, (N_EXPERTS, LANES)).astype(I32)


def _router(h_all, wr_t_bf, bias_col, tt):
    n = h_all.shape[0]
    tokrow = pl.BlockSpec((TOP_K, tt), lambda i: (0, i))
    return pl.pallas_call(
        _router_kernel, grid=(n // tt,),
        in_specs=[pl.BlockSpec((tt, D_MODEL), lambda i: (i, 0)),
                  pl.BlockSpec((N_EXPERTS, D_MODEL), lambda i: (0, 0)),
                  pl.BlockSpec((N_EXPERTS, 1), lambda i: (0, 0))],
        out_specs=[tokrow, tokrow, tokrow, pl.BlockSpec((N_EXPERTS, LANES), lambda i: (0, 0))],
        out_shape=[jax.ShapeDtypeStruct((TOP_K, n), I32), jax.ShapeDtypeStruct((TOP_K, n), F32),
                   jax.ShapeDtypeStruct((TOP_K, n), I32), jax.ShapeDtypeStruct((N_EXPERTS, LANES), I32)],
        scratch_shapes=[pltpu.VMEM((N_EXPERTS, 1), F32)],
        compiler_params=_cparams(("arbitrary",)), name="router")(h_all, wr_t_bf, bias_col)


def _positions_kernel(idx_ref, rank_ref, pstart_ref, dest_ref):
    tt = idx_ref.shape[1]
    io_e = lax.broadcasted_iota(I32, (N_EXPERTS, tt), 0).astype(F32)
    idx = idx_ref[...].astype(F32)
    first_row = pstart_ref[...].astype(F32)
    rows = [jnp.sum(jnp.where(io_e == idx[k:k + 1, :], first_row, 0.0), axis=0, keepdims=True)
            for k in range(TOP_K)]
    dest_ref[...] = jnp.concatenate(rows, axis=0).astype(I32) + rank_ref[...]


def _positions(idx_t, rank_t, pstart_col, tt):
    n = idx_t.shape[1]
    tok = pl.BlockSpec((TOP_K, tt), lambda i: (0, i))
    return pl.pallas_call(
        _positions_kernel, grid=(n // tt,),
        in_specs=[tok, tok, pl.BlockSpec((N_EXPERTS, 1), lambda i: (0, 0))],
        out_specs=tok, out_shape=jax.ShapeDtypeStruct((TOP_K, n), I32),
        compiler_params=_cparams(("arbitrary",)), name="positions")(idx_t, rank_t, pstart_col)


def _dispatch_kernel(dest_ref, hp_ref, xs_ref, sem):
    tt = dest_ref.shape[1]

    def row_copy(src_row, dst_row):
        return pltpu.make_async_copy(hp_ref.at[pl.ds(pl.multiple_of(src_row * SLABS, SLABS), SLABS), :],
                                     xs_ref.at[pl.ds(pl.multiple_of(dst_row * SLABS, SLABS), SLABS), :], sem)

    def issue(t, carry):
        for k in range(TOP_K):
            row_copy(t, dest_ref[k, t]).start(priority=k % 2)
        return carry

    def drain(t, carry):
        for k in range(TOP_K):
            row_copy(0, 0).wait()
        return carry

    lax.fori_loop(0, tt, issue, 0)
    lax.fori_loop(0, tt, drain, 0)


def _dispatch(dest, hp, n_rows_padded, tt):
    n = dest.shape[1]
    return pl.pallas_call(
        _dispatch_kernel, grid=(n // tt,),
        in_specs=[pl.BlockSpec((TOP_K, tt), lambda i: (0, i), memory_space=pltpu.SMEM),
                  pl.BlockSpec((tt * SLABS, LANES), lambda i: (i, 0))],
        out_specs=pl.BlockSpec(memory_space=pl.ANY),
        out_shape=jax.ShapeDtypeStruct((n_rows_padded * SLABS, LANES), BF16),
        scratch_shapes=[pltpu.SemaphoreType.DMA],
        compiler_params=_cparams(("arbitrary",)), name="dispatch")(dest, hp)


UP_OFF = -(-D_EXPERT // LANES) * LANES


def _experts_kernel(se_ref, sn_ref, nu_ref, send_ref, xs_ref, wg_hbm, wu_hbm, wd_hbm, ys_ref,
                    wg_f, wu_f, wd_f, wgu_s, wd_s, xf_scr, sems, ord_ref):
    b = pl.program_id(0)
    nu = nu_ref[0]
    last = pl.num_programs(0) - 1
    e = se_ref[b]
    live = b < nu
    fresh = (b == 0) | (e != se_ref[jnp.maximum(b - 1, 0)])
    srcs, bufs = (wg_hbm, wu_hbm, wd_hbm), (wg_f, wu_f, wd_f)

    def fetch(j, expert, slot):
        return pltpu.make_async_copy(srcs[j].at[expert], bufs[j].at[slot], sems.at[slot, j])

    def stage(j, slot):
        if j == 0:
            wgu_s[0:D_EXPERT, :] = wg_f[slot].astype(BF16)
        elif j == 1:
            wgu_s[UP_OFF:UP_OFF + D_EXPERT, :] = wu_f[slot].astype(BF16)
        else:
            wd_s[...] = wd_f[slot].astype(BF16)

    nxt1 = send_ref[e]
    e1 = se_ref[jnp.minimum(nxt1, last)]
    nxt2 = jnp.where(nxt1 < nu, send_ref[e1], nu)
    e2 = se_ref[jnp.minimum(nxt2, last)]

    @pl.when(b == 0)
    def _():
        ord_ref[0] = 0
        wgu_s[D_EXPERT:UP_OFF, :] = jnp.zeros((UP_OFF - D_EXPERT, D_MODEL), BF16)
        for j in range(3):
            fetch(j, e, 0).start(priority=WEIGHT_DMA_PRIORITY)

        @pl.when(nxt1 < nu)
        def _():
            for j in range(3):
                fetch(j, e1, 1).start(priority=WEIGHT_DMA_PRIORITY)

    @pl.when(live & fresh)
    def _():
        slot = ord_ref[0] % 2
        for j in range(3):
            fetch(j, e, slot).wait()
            stage(j, slot)

            @pl.when(nxt2 < nu)
            def _():
                fetch(j, e2, slot).start(priority=WEIGHT_DMA_PRIORITY)

        ord_ref[0] = ord_ref[0] + 1

    blk_rows = ROW_BLK * SLABS

    def ffn(n_blk):
        m = n_blk * ROW_BLK
        x = jnp.concatenate([xf_scr[pl.ds(s, m, stride=SLABS), :].astype(BF16) for s in range(SLABS)],
                            axis=1)
        gu = lax.dot_general(x, wgu_s[...], (((1,), (1,)), ((), ())), preferred_element_type=F32)
        g, u = gu[:, :D_EXPERT], gu[:, UP_OFF:UP_OFF + D_EXPERT]
        hm = (g * _sigmoid(g)) * u
        return jnp.dot(hm.astype(BF16), wd_s[...], preferred_element_type=F32)

    def store(n_blk, y):
        for s in range(SLABS):
            ys_ref[pl.ds(s, n_blk * ROW_BLK, stride=SLABS), :] = y[:, s * LANES:(s + 1) * LANES]

    for n_blk in range(1, STEP_BLKS + 1):
        @pl.when(live & (sn_ref[b] == n_blk))
        def _():
            rows = n_blk * blk_rows
            xf_scr[0:rows, :] = xs_ref[0:rows, :].astype(F32)
            store(n_blk, ffn(n_blk))


def _experts(step_expert, step_nblk, n_used, step_end, xs, wg_t, wu_t, wd, n_steps):
    wshape = (D_EXPERT, D_MODEL)
    window = pl.BlockSpec((STEP_ROWS * SLABS, LANES), lambda b, se, sn, nu, sd: (jnp.minimum(b, nu[0] - 1), 0))
    grid_spec = pltpu.PrefetchScalarGridSpec(
        num_scalar_prefetch=4, grid=(n_steps,),
        in_specs=[window] + [pl.BlockSpec(memory_space=pl.ANY)] * 3,
        out_specs=window,
        scratch_shapes=[pltpu.VMEM((2,) + wshape, F32)] * 3
        + [pltpu.VMEM((UP_OFF + D_EXPERT, D_MODEL), BF16), pltpu.VMEM(wshape, BF16),
           pltpu.VMEM((STEP_ROWS * SLABS, LANES), F32), pltpu.SemaphoreType.DMA((2, 3)), pltpu.SMEM((1,), I32)])
    return pl.pallas_call(
        _experts_kernel, grid_spec=grid_spec,
        out_shape=jax.ShapeDtypeStruct((n_steps * STEP_ROWS * SLABS, LANES), F32),
        compiler_params=_cparams(("arbitrary",), vmem_mb=56), name="experts")(
            step_expert, step_nblk, n_used, step_end, xs, wg_t, wu_t, wd)


def _combine_kernel(dest_ref, h_ref, gate_ref, swg_ref, swu_ref, swd_ref, g_ref, b_ref,
                    ys_ref, yp_ref, ysm_ref, gbuf, sem, *, n_prompt_tiles):
    i = pl.program_id(0)
    tt = h_ref.shape[0]

    def row_copy(src_row, dst_row):
        return pltpu.make_async_copy(
            ys_ref.at[pl.ds(pl.multiple_of(src_row * SLABS, SLABS), SLABS), :],
            gbuf.at[pl.ds(pl.multiple_of(dst_row * SLABS, SLABS), SLABS), :], sem)

    def issue(t, carry):
        for k in range(TOP_K):
            row_copy(dest_ref[k, t], k * tt + t).start(priority=k % 2)
        return carry

    def drain(t, carry):
        for k in range(TOP_K):
            row_copy(0, 0).wait()
        return carry

    lax.fori_loop(0, tt, issue, 0)
    h = h_ref[...]
    x = h.astype(BF16)
    nt = (((1,), (1,)), ((), ()))
    gt = lax.dot_general(x, swg_ref[...], nt, preferred_element_type=F32)
    up = lax.dot_general(x, swu_ref[...], nt, preferred_element_type=F32)
    shared = jnp.dot(((gt * _sigmoid(gt)) * up).astype(BF16), swd_ref[...], preferred_element_type=F32)
    lax.fori_loop(0, tt, drain, 0)
    gates = gate_ref[...]
    cols = []
    for s in range(SLABS):
        acc = jnp.zeros((tt, LANES), F32)
        for k in range(TOP_K):
            acc = acc + gates[:, k:k + 1] * gbuf[pl.ds(k * tt * SLABS + s, tt, stride=SLABS), :]
        cols.append(acc)
    routed = jnp.concatenate(cols, axis=1)
    y = _layer_norm(ALPHA * h + (routed + shared), g_ref[...], b_ref[...])

    @pl.when(i < n_prompt_tiles)
    def _():
        yp_ref[...] = y

    @pl.when(i >= n_prompt_tiles)
    def _():
        ysm_ref[...] = y


def _combine(dest, h_all, gates_tok, swg_t, swu_t, swd, g, b, ys, n_prompt, tt):
    n = h_all.shape[0]
    npt = n_prompt // tt
    vec = pl.BlockSpec((1, D_MODEL), lambda i: (0, 0))
    full = lambda a: pl.BlockSpec(a.shape, lambda i: (0, 0))
    return pl.pallas_call(
        functools.partial(_combine_kernel, n_prompt_tiles=npt), grid=(n // tt,),
        in_specs=[pl.BlockSpec((TOP_K, tt), lambda i: (0, i), memory_space=pltpu.SMEM),
                  pl.BlockSpec((tt, D_MODEL), lambda i: (i, 0)),
                  pl.BlockSpec((tt, TOP_K), lambda i: (i, 0)),
                  full(swg_t), full(swu_t), full(swd), vec, vec,
                  pl.BlockSpec(memory_space=pl.ANY)],
        out_specs=[pl.BlockSpec((tt, D_MODEL), lambda i: (jnp.minimum(i, npt - 1), 0)),
                   pl.BlockSpec((tt, D_MODEL), lambda i: (jnp.maximum(i - npt, 0), 0))],
        out_shape=[jax.ShapeDtypeStruct((n_prompt, D_MODEL), F32),
                   jax.ShapeDtypeStruct((n - n_prompt, D_MODEL), F32)],
        scratch_shapes=[pltpu.VMEM((TOP_K * tt * SLABS, LANES), F32), pltpu.SemaphoreType.DMA],
        compiler_params=_cparams(("arbitrary",)), name="combine_shared_ln")(
            dest, h_all, gates_tok, swg_t, swu_t, swd, g, b, ys)


def _mixer(x2d, n_seq, seq_len, s0_bd, shift0, conv0, wts, tm, tn_r, tt_prep, chunk, wkv_nb, tt_conv):
    proj_r = _matmul(x2d, wts["w_r"], tm, tn_r, "inproj_rwkv")
    u = _glu_matmul(x2d, wts["w_ca"], wts["w_cb"], tm, "inproj_glu")
    streams = _rwkv_prep(proj_r, shift0, wts["prep_vecs"], wts["prep_mats"], n_seq, seq_len, tt_prep)
    o_rwkv, s_fin = _wkv(streams, wts["gn_g"], wts["gn_b"], s0_bd, n_seq, seq_len, chunk, wkv_nb)
    conv0_pad = jnp.pad(conv0, ((0, 0), (HALO - (CONV_WIDTH - 1), 0), (0, 0)))
    o_conv = _conv_module(u, conv0_pad, wts["conv_w"], wts["conv_b"], wts["conv_ln_g"], wts["conv_ln_b"],
                          n_seq, seq_len, tt_conv)
    new_shift = proj_r.reshape(n_seq, seq_len, D_SHIFT)[:, seq_len - 1:, :]
    full = jnp.concatenate([conv0, u.reshape(n_seq, seq_len, D_CONV)], axis=1)
    new_conv = full[:, full.shape[1] - (CONV_WIDTH - 1):]
    return o_rwkv, o_conv, _from_block_diag(s_fin), new_shift, new_conv


def kernel(x_prompt, x_sample, state_wkv, state_shift, state_conv, w_in, mu_shift, w0, w2, a0, a2, g2, k_k, k_a, r_k, gn_g, gn_b, conv_w, conv_b, conv_ln_g, conv_ln_b, w_out, ln1_g, ln1_b, w_router, router_bias, exp_w_gate, exp_w_up, exp_w_down, sh_w_gate, sh_w_up, sh_w_down, ln2_g, ln2_b):
    bp, tp, _ = x_prompt.shape
    bs, ts, _ = x_sample.shape
    n_p, n_s = bp * tp, bs * ts
    n_all = n_p + n_s
    row = lambda v: v.reshape(1, -1)
    zpad = jnp.zeros((64, D_RWKV), F32)
    wts = {
        "w_r": w_in[0][:, :D_SHIFT].astype(BF16),
        "w_ca": w_in[0][:, D_SHIFT:D_SHIFT + D_CONV].astype(BF16),
        "w_cb": w_in[0][:, D_SHIFT + D_CONV:].astype(BF16),
        "prep_vecs": [row(mu_shift[0]), row(w0[0]), row(a0[0]), row(k_k[0]), row(k_a[0]), row(r_k[0])],
        "prep_mats": [jnp.concatenate([w2[0], zpad], axis=0).astype(BF16),
                      jnp.concatenate([zpad, a2[0]], axis=0).astype(BF16),
                      g2[0].astype(BF16)],
        "gn_g": row(gn_g[0]), "gn_b": row(gn_b[0]),
        "conv_w": conv_w[0], "conv_b": row(conv_b[0]),
        "conv_ln_g": row(conv_ln_g[0]), "conv_ln_b": row(conv_ln_b[0]),
    }
    w_out_bf = w_out[0].astype(BF16)
    g1, b1 = row(ln1_g[0]), row(ln1_b[0])

    xp = x_prompt.reshape(n_p, D_MODEL)
    zero_s = jnp.zeros((bp, N_PAIRS, LANES, LANES), F32)
    zero_shift = jnp.zeros((bp, 1, D_SHIFT), F32)
    zero_conv = jnp.zeros((bp, CONV_WIDTH - 1, D_CONV), F32)
    orw_p, ocv_p, wkv_p, shift_p, conv_p = _mixer(xp, bp, tp, zero_s, zero_shift, zero_conv, wts,
                                                  tm=512, tn_r=D_SHIFT // 2, tt_prep=256, chunk=64, wkv_nb=bp,
                                                  tt_conv=128)
    xsm = x_sample.reshape(n_s, D_MODEL)
    orw_s, ocv_s, wkv_s, shift_s, conv_s = _mixer(xsm, bs, ts, _to_block_diag(state_wkv[0]), state_shift[0],
                                                  state_conv[0], wts, tm=n_s, tn_r=D_SHIFT // 2, tt_prep=ts,
                                                  chunk=ts, wkv_nb=1, tt_conv=ts)
    h_all, hp_all = _outproj_ln(orw_p, ocv_p, xp, w_out_bf, g1, b1, 256, n_all, 0)
    h_all, hp_all = _outproj_ln(orw_s, ocv_s, xsm, w_out_bf, g1, b1, n_s, n_all, n_p, prev=(h_all, hp_all))

    tt = TOK_TILE
    idx_t, gate_t, rank_t, cnt = _router(h_all, w_router[0].T.astype(BF16), router_bias[0].reshape(-1, 1), tt)
    counts = cnt[:, 0]
    steps = (counts + STEP_ROWS - 1) // STEP_ROWS
    step_end = jnp.cumsum(steps).astype(I32)
    step_start = step_end - steps
    pstart = step_start * STEP_ROWS
    n_steps = _max_steps(n_all * TOP_K)
    n_used = step_end[-1:]
    step = jnp.minimum(jnp.arange(n_steps, dtype=I32), step_end[-1] - 1)
    step_expert = jnp.minimum(jnp.sum((step_end[None, :] <= step[:, None]).astype(I32), axis=1), N_EXPERTS - 1)
    mine = step_expert[:, None] == jnp.arange(N_EXPERTS, dtype=I32)[None, :]
    rows_left = jnp.sum(jnp.where(mine, counts[None, :] - (step[:, None] - step_start[None, :]) * STEP_ROWS, 0),
                        axis=1)
    step_nblk = jnp.clip((rows_left + ROW_BLK - 1) // ROW_BLK, 1, STEP_BLKS).astype(I32)
    dest = _positions(idx_t, rank_t, pstart.reshape(-1, 1), 5 * TOK_TILE)
    xs = _dispatch(dest, hp_all, n_steps * STEP_ROWS, tt)
    ys = _experts(step_expert, step_nblk, n_used, step_end, xs, jnp.swapaxes(exp_w_gate[0], 1, 2),
                  jnp.swapaxes(exp_w_up[0], 1, 2), exp_w_down[0], n_steps)
    y_p, y_s = _combine(dest, h_all, gate_t.T, sh_w_gate[0].T.astype(BF16),
                        sh_w_up[0].T.astype(BF16), sh_w_down[0].astype(BF16), row(ln2_g[0]), row(ln2_b[0]),
                        ys, n_p, tt)
    return (y_p.reshape(bp, tp, D_MODEL), y_s.reshape(bs, ts, D_MODEL),
            wkv_p[None], shift_p[None], conv_p[None], wkv_s[None], shift_s[None], conv_s[None])
```

```python
import functools
import math

import jax
import jax.numpy as jnp
from jax import lax
from jax.experimental import pallas as pl
from jax.experimental.pallas import tpu as pltpu

F32 = jnp.float32
BF16 = jnp.bfloat16
I32 = jnp.int32
HIGHEST = lax.Precision.HIGHEST

D_MODEL = 2048
D_RWKV = 1024
D_CONV = 1024
HEAD_DIM = 64
N_HEADS = D_RWKV // HEAD_DIM
N_PAIRS = N_HEADS // 2
CONV_WIDTH = 31
D_SHIFT = 3 * D_RWKV + 64 + 64 + 128
N_EXPERTS = 256
TOP_K = 8
N_GROUPS = 8
GROUP_SIZE = N_EXPERTS // N_GROUPS
TOPK_GROUPS = 4
D_EXPERT = 576
ROUTED_SCALE = 2.5
ALPHA = 2.0 ** 0.25
LN_EPS = 1e-5
GN_EPS = 64e-5

LANES = 128
ROW_BLK = 64
STEP_BLKS = 6
STEP_ROWS = STEP_BLKS * ROW_BLK
TOK_TILE = 128
SLABS = D_MODEL // LANES
WEIGHT_DMA_PRIORITY = 1


def _max_steps(n_rows):
    return -(-(n_rows + N_EXPERTS * (STEP_ROWS - 1)) // STEP_ROWS)


def _cparams(sem, vmem_mb=48):
    return pltpu.CompilerParams(dimension_semantics=sem, vmem_limit_bytes=vmem_mb * 2 ** 20)


def _sigmoid(x):
    return 1.0 / (1.0 + jnp.exp(-x))


def _dot(a, b):
    return jnp.dot(a.astype(BF16), b.astype(BF16), preferred_element_type=F32)


def _dot_nt(a, b):
    return lax.dot_general(a.astype(BF16), b.astype(BF16), (((1,), (1,)), ((), ())),
                           preferred_element_type=F32)


def _dot_tn(a, b):
    return lax.dot_general(a.astype(BF16), b.astype(BF16), (((0,), (0,)), ((), ())),
                           preferred_element_type=F32)


def _head_ones():
    r = lax.broadcasted_iota(I32, (LANES, LANES), 0) // HEAD_DIM
    c = lax.broadcasted_iota(I32, (LANES, LANES), 1) // HEAD_DIM
    return (r == c).astype(F32)


def _segsum(x, ones_bd):
    outs = [jnp.dot(x[:, s * LANES:(s + 1) * LANES], ones_bd, precision=HIGHEST,
                    preferred_element_type=F32) for s in range(x.shape[1] // LANES)]
    return outs[0] if len(outs) == 1 else jnp.concatenate(outs, axis=1)


def _layer_norm(x, g, b):
    mu = jnp.mean(x, axis=-1, keepdims=True)
    d = x - mu
    var = jnp.mean(d * d, axis=-1, keepdims=True)
    return d * lax.rsqrt(var + LN_EPS) * g + b


def _mm_kernel(x_ref, w_ref, o_ref):
    o_ref[...] = jnp.dot(x_ref[...].astype(BF16), w_ref[...], preferred_element_type=F32)


def _matmul(x, w_bf, tm, tn, name):
    m, k = x.shape
    n = w_bf.shape[1]
    return pl.pallas_call(
        _mm_kernel, grid=(n // tn, m // tm),
        in_specs=[pl.BlockSpec((tm, k), lambda j, i: (i, 0)),
                  pl.BlockSpec((k, tn), lambda j, i: (0, j))],
        out_specs=pl.BlockSpec((tm, tn), lambda j, i: (i, j)),
        out_shape=jax.ShapeDtypeStruct((m, n), F32),
        compiler_params=_cparams(("arbitrary", "arbitrary")), name=name)(x, w_bf)


def _glu_kernel(x_ref, wa_ref, wb_ref, o_ref):
    x = x_ref[...].astype(BF16)
    a = jnp.dot(x, wa_ref[...], preferred_element_type=F32)
    b = jnp.dot(x, wb_ref[...], preferred_element_type=F32)
    o_ref[...] = a * _sigmoid(b)


def _glu_matmul(x, wa_bf, wb_bf, tm, name):
    m, k = x.shape
    n = wa_bf.shape[1]
    return pl.pallas_call(
        _glu_kernel, grid=(m // tm,),
        in_specs=[pl.BlockSpec((tm, k), lambda i: (i, 0)),
                  pl.BlockSpec((k, n), lambda i: (0, 0)),
                  pl.BlockSpec((k, n), lambda i: (0, 0))],
        out_specs=pl.BlockSpec((tm, n), lambda i: (i, 0)),
        out_shape=jax.ShapeDtypeStruct((m, n), F32),
        compiler_params=_cparams(("arbitrary",)), name=name)(x, wa_bf, wb_bf)


def _prep_kernel(p_ref, pv_ref, sh_ref, mu_ref, w0_ref, a0_ref, kk_ref, ka_ref, rk_ref,
                 w2_ref, a2_ref, g2_ref,
                 r_out, lw_out, k_out, v_out, kn_out, b_out, g_out, bonus_out):
    i = pl.program_id(1)
    p = p_ref[...]
    tt = p.shape[0]
    carry = jnp.where(i == 0, sh_ref[...], pv_ref[7:8, :])
    row = lax.broadcasted_iota(I32, (tt, 1), 0)
    prev = jnp.where(row == 0, carry, pltpu.roll(p, 1, 0))
    xs = p + (prev - p) * mu_ref[...]
    r = xs[:, 0:D_RWKV]
    k = xs[:, D_RWKV:2 * D_RWKV]
    v = xs[:, 2 * D_RWKV:3 * D_RWKV]
    wa = xs[:, 3 * D_RWKV:3 * D_RWKV + 128]
    gd = xs[:, 3 * D_RWKV + 128:]
    z = w0_ref[...] + _dot(jnp.tanh(wa), w2_ref[...])
    w_log = -(jnp.maximum(-z, 0.0) + jnp.log1p(jnp.exp(-jnp.abs(z)))) - 0.5
    lw_out[...] = -jnp.exp(w_log)
    a = _sigmoid(a0_ref[...] + _dot(wa, a2_ref[...]))
    g_out[...] = _dot(_sigmoid(gd), g2_ref[...])
    ones_bd = _head_ones()
    kk = k * kk_ref[...]
    nrm = jnp.sqrt(_segsum(kk * kk, ones_bd))
    kn = kk / jnp.maximum(nrm, 1e-12)
    kh = k * (1.0 + (a - 1.0) * ka_ref[...])
    r_out[...] = r
    k_out[...] = kh
    v_out[...] = v
    kn_out[...] = kn
    b_out[...] = kn * a
    bonus_out[...] = _segsum(r * kh * rk_ref[...], ones_bd) * v


def _rwkv_prep(proj, shift0, vecs, mats, n_seq, seq_len, tt):
    nt = seq_len // tt
    n = n_seq * seq_len
    row_vec = lambda c: pl.BlockSpec((1, c), lambda b, i: (0, 0))
    full = lambda a: pl.BlockSpec(a.shape, lambda b, i: (0, 0))
    out_spec = pl.BlockSpec((tt, D_RWKV), lambda b, i: (b * nt + i, 0))
    outs = pl.pallas_call(
        _prep_kernel, grid=(n_seq, nt),
        in_specs=[pl.BlockSpec((tt, D_SHIFT), lambda b, i: (b * nt + i, 0)),
                  pl.BlockSpec((8, D_SHIFT), lambda b, i: (jnp.maximum((b * nt + i) * (tt // 8) - 1, 0), 0)),
                  pl.BlockSpec((None, 1, D_SHIFT), lambda b, i: (b, 0, 0)),
                  row_vec(D_SHIFT)] + [row_vec(D_RWKV)] * 5 + [full(m) for m in mats],
        out_specs=[out_spec] * 8,
        out_shape=[jax.ShapeDtypeStruct((n, D_RWKV), F32)] * 8,
        compiler_params=_cparams(("arbitrary", "arbitrary")), name="rwkv_prep")(
            proj, proj, shift0, *vecs, *mats)
    return outs


def _wkv_kernel(r_ref, lw_ref, k_ref, v_ref, kn_ref, b_ref, g_ref, bonus_ref, gng_ref, gnb_ref, s0_ref,
                o_ref, sfin_ref, s_scr):
    c = pl.program_id(1)
    nc = pl.num_programs(1)
    nb, C = r_ref.shape[0], r_ref.shape[1]
    C2 = 2 * C
    fused = C2 == LANES
    pairs = range(nb * N_PAIRS)

    @pl.when(c == 0)
    def _():
        s_scr[...] = s0_ref[...]

    lane = lax.broadcasted_iota(I32, (C, LANES), 1)
    m0 = lane < HEAD_DIM
    ri = lax.broadcasted_iota(I32, (C2, C2), 0)
    ci = lax.broadcasted_iota(I32, (C2, C2), 1)
    same = (ri // C) == (ci // C)
    strict = same & (ri > ci)
    incl = same & (ri >= ci)
    eye = (ri == ci).astype(F32)
    tr = lax.broadcasted_iota(I32, (C, C), 0)
    tc = lax.broadcasted_iota(I32, (C, C), 1)
    ltri = (tr >= tc).astype(BF16)
    ones_bd = _head_ones().astype(BF16)
    n_sq = int(math.log2(C)) - 1

    def stack(x):
        return jnp.concatenate([jnp.where(m0, x, 0.0), jnp.where(m0, 0.0, x)], axis=0).astype(BF16)

    def split_dot(lhs_bf, x, rhs_bf, terms):
        acc, rem = None, x
        for _ in range(terms):
            piece = rem.astype(BF16)
            part = (jnp.dot(lhs_bf, piece, preferred_element_type=F32) if rhs_bf is None
                    else jnp.dot(piece, rhs_bf, preferred_element_type=F32))
            acc = part if acc is None else acc + part
            rem = rem - piece.astype(F32)
        return acc

    lw_all = [lw_ref[q] for q in range(nb)]
    lc_all = [split_dot(ltri, lw, None, 3) for lw in lw_all]
    sls = [slice((p % N_PAIRS) * LANES, (p % N_PAIRS + 1) * LANES) for p in pairs]
    seq = [p // N_PAIRS for p in pairs]

    ar, bk, vst, endst, decay_end = [], [], [], [], []
    for p in pairs:
        sl, q = sls[p], seq[p]
        lw, lc = lw_all[q][:, sl], lc_all[q][:, sl]
        lc_end = lc[C - 1:C, :]
        e_neg = jnp.exp(-lc)
        e_end = jnp.exp(lc_end - lc)
        bb, kh = b_ref[q, :, sl], k_ref[q, :, sl]
        ar.append(jnp.concatenate([stack(-kn_ref[q, :, sl] * jnp.exp(lc - lw)),
                                   stack(r_ref[q, :, sl] * jnp.exp(lc))], axis=0))
        bk.append(jnp.concatenate([stack(bb * e_neg), stack(kh * e_neg)], axis=0))
        vst.append(stack(v_ref[q, :, sl]))
        endst.append(jnp.concatenate([stack(bb * e_end), stack(kh * e_end)], axis=0))
        decay_end.append(jnp.exp(lc_end))

    ab, ak, rbk = [], [], []
    for p in pairs:
        if fused:
            m = _dot_nt(ar[p], bk[p])
            ab.append(jnp.where(strict, m[:C2, :C2], 0.0))
            ak.append(jnp.where(strict, m[:C2, C2:], 0.0).astype(BF16))
            rbk.append(jnp.concatenate([jnp.where(incl, m[C2:, :C2], 0.0),
                                        jnp.where(incl, m[C2:, C2:], 0.0)], axis=1).astype(BF16))
        else:
            a_st, r_st, b_st, k_st = ar[p][:C2], ar[p][C2:], bk[p][:C2], bk[p][C2:]
            ab.append(jnp.where(strict, _dot_nt(a_st, b_st), 0.0))
            ak.append(jnp.where(strict, _dot_nt(a_st, k_st), 0.0).astype(BF16))
            rbk.append((jnp.where(incl, _dot_nt(r_st, b_st), 0.0).astype(BF16),
                        jnp.where(incl, _dot_nt(r_st, k_st), 0.0).astype(BF16)))

    tm = [eye + ab[p] for p in pairs]
    pw = [_dot(ab[p], ab[p]) for p in pairs]
    for _ in range(n_sq - 1):
        for p in pairs:
            both = _dot(jnp.concatenate([tm[p], pw[p]], axis=0), pw[p])
            tm[p] = tm[p] + both[:C2]
            pw[p] = both[C2:]
    for p in pairs:
        tm[p] = (tm[p] + _dot(tm[p], pw[p])).astype(BF16)

    s_old = [s_scr[seq[p], p % N_PAIRS] for p in pairs]
    ars = [_dot_nt(ar[p], s_old[p]) for p in pairs]
    rhs = [ars[p][:C2] + jnp.dot(ak[p], vst[p], preferred_element_type=F32) for p in pairs]
    uv = [jnp.concatenate([jnp.dot(tm[p], rhs[p].astype(BF16), preferred_element_type=F32).astype(BF16),
                           vst[p]], axis=0) for p in pairs]
    for p in pairs:
        s_scr[seq[p], p % N_PAIRS] = s_old[p] * decay_end[p] + _dot_tn(uv[p], endst[p])
    for p in pairs:
        sl, q = sls[p], seq[p]
        if fused:
            o_st = ars[p][C2:] + jnp.dot(rbk[p], uv[p], preferred_element_type=F32)
        else:
            o_st = (ars[p][C2:] + jnp.dot(rbk[p][0], uv[p][:C2], preferred_element_type=F32)
                    + jnp.dot(rbk[p][1], uv[p][C2:], preferred_element_type=F32))
        o = o_st[:C] + o_st[C:]
        mu = split_dot(None, o, ones_bd, 2) * (1.0 / HEAD_DIM)
        d = o - mu
        var = split_dot(None, d * d, ones_bd, 2) * (1.0 / HEAD_DIM)
        on = d * lax.rsqrt(var + GN_EPS) * gng_ref[:, sl] + gnb_ref[:, sl]
        o_ref[q, :, sl] = (on + bonus_ref[q, :, sl]) * g_ref[q, :, sl]

    @pl.when(c == nc - 1)
    def _():
        sfin_ref[...] = s_scr[...]


def _wkv(streams, gn_g, gn_b, s0_bd, n_seq, seq_len, chunk, nb):
    nc = seq_len // chunk
    tok = pl.BlockSpec((nb, chunk, D_RWKV), lambda b, c: (b, c, 0))
    vec = pl.BlockSpec((1, D_RWKV), lambda b, c: (0, 0))
    st = pl.BlockSpec((nb, N_PAIRS, LANES, LANES), lambda b, c: (b, 0, 0, 0))
    o, s_fin = pl.pallas_call(
        _wkv_kernel, grid=(n_seq // nb, nc),
        in_specs=[tok] * 8 + [vec, vec, st],
        out_specs=[tok, st],
        out_shape=[jax.ShapeDtypeStruct((n_seq, seq_len, D_RWKV), F32),
                   jax.ShapeDtypeStruct((n_seq, N_PAIRS, LANES, LANES), F32)],
        scratch_shapes=[pltpu.VMEM((nb, N_PAIRS, LANES, LANES), F32)],
        compiler_params=_cparams(("arbitrary", "arbitrary")), name="wkv_chunks")(
            *[s.reshape(n_seq, seq_len, D_RWKV) for s in streams], gn_g, gn_b, s0_bd)
    return o.reshape(n_seq * seq_len, D_RWKV), s_fin


def _to_block_diag(s):
    b = s.shape[0]
    s = s.reshape(b, N_PAIRS, 2, HEAD_DIM, HEAD_DIM)
    z = jnp.zeros_like(s[:, :, 0])
    top = jnp.concatenate([s[:, :, 0], z], axis=-1)
    bot = jnp.concatenate([z, s[:, :, 1]], axis=-1)
    return jnp.concatenate([top, bot], axis=-2)


def _from_block_diag(bd):
    b = bd.shape[0]
    h0 = bd[:, :, :HEAD_DIM, :HEAD_DIM]
    h1 = bd[:, :, HEAD_DIM:, HEAD_DIM:]
    return jnp.stack([h0, h1], axis=2).reshape(b, N_HEADS, HEAD_DIM, HEAD_DIM)


HALO = 32


def _conv_kernel(u_ref, halo_ref, cp_ref, cw_ref, cb_ref, lg_ref, lb_ref, o_ref, xbuf, ybuf, wbuf):
    i = pl.program_id(1)
    tt = u_ref.shape[0]
    xbuf[0:HALO, :] = jnp.where(i == 0, cp_ref[...], halo_ref[...])
    xbuf[HALO:HALO + tt, :] = u_ref[...]
    for cs in range(D_CONV // LANES):
        sl = slice(cs * LANES, (cs + 1) * LANES)
        for r in range(8):
            wbuf[r] = xbuf[8 - r:8 - r + tt + 24, sl]
        acc = jnp.zeros((tt, LANES), F32)
        for r in range(8):
            for q in range(4):
                s = 8 * q + r
                if s > CONV_WIDTH - 1:
                    continue
                j = CONV_WIDTH - 1 - s
                acc = acc + cw_ref[j:j + 1, sl] * wbuf[r, 24 - 8 * q:24 - 8 * q + tt, :]
        ybuf[:, sl] = acc + cb_ref[:, sl]
    y = _layer_norm(ybuf[...], lg_ref[...], lb_ref[...])
    o_ref[...] = y * _sigmoid(y)


def _conv_module(u, conv0_pad, cw, cb, lg, lb, n_seq, seq_len, tt):
    nt = seq_len // tt
    n = n_seq * seq_len
    vec = pl.BlockSpec((1, D_CONV), lambda b, i: (0, 0))
    return pl.pallas_call(
        _conv_kernel, grid=(n_seq, nt),
        in_specs=[pl.BlockSpec((tt, D_CONV), lambda b, i: (b * nt + i, 0)),
                  pl.BlockSpec((HALO, D_CONV),
                               lambda b, i: (jnp.maximum((b * seq_len + i * tt) // HALO - 1, 0), 0)),
                  pl.BlockSpec((None, HALO, D_CONV), lambda b, i: (b, 0, 0)),
                  pl.BlockSpec((CONV_WIDTH, D_CONV), lambda b, i: (0, 0)), vec, vec, vec],
        out_specs=pl.BlockSpec((tt, D_CONV), lambda b, i: (b * nt + i, 0)),
        out_shape=jax.ShapeDtypeStruct((n, D_CONV), F32),
        scratch_shapes=[pltpu.VMEM((HALO + tt, D_CONV), F32), pltpu.VMEM((tt, D_CONV), F32),
                        pltpu.VMEM((8, tt + 24, LANES), F32)],
        compiler_params=_cparams(("arbitrary", "arbitrary")), name="conv_module")(
            u, u, conv0_pad, cw, cb, lg, lb)


def _outproj_kernel(*refs, aliased):
    if aliased:
        orw_ref, ocv_ref, x_ref, w_ref, g_ref, b_ref, _, _, h_ref, hp_ref, tile_scr = refs
    else:
        orw_ref, ocv_ref, x_ref, w_ref, g_ref, b_ref, h_ref, hp_ref, tile_scr = refs
    tm = x_ref.shape[0]
    mix = (jnp.dot(orw_ref[...].astype(BF16), w_ref[0:D_RWKV, :], preferred_element_type=F32)
           + jnp.dot(ocv_ref[...].astype(BF16), w_ref[D_RWKV:, :], preferred_element_type=F32))
    h = _layer_norm(ALPHA * x_ref[...] + mix, g_ref[...], b_ref[...])
    h_ref[...] = h
    for s in range(SLABS):
        tile_scr[pl.ds(s, tm, stride=SLABS), :] = h[:, s * LANES:(s + 1) * LANES]
    hp_ref[...] = tile_scr[...].astype(BF16)


def _outproj_ln(o_rwkv, o_conv, x, w_bf, g, b, tm, n_total, row_off, prev=None):
    m = x.shape[0]
    blk_off = row_off // tm
    vec = pl.BlockSpec((1, D_MODEL), lambda i: (0, 0))
    in_specs = [pl.BlockSpec((tm, D_RWKV), lambda i: (i, 0)),
                pl.BlockSpec((tm, D_CONV), lambda i: (i, 0)),
                pl.BlockSpec((tm, D_MODEL), lambda i: (i, 0)),
                pl.BlockSpec((D_MODEL, D_MODEL), lambda i: (0, 0)), vec, vec]
    args = [o_rwkv, o_conv, x, w_bf, g, b]
    aliases = {}
    if prev is not None:
        in_specs += [pl.BlockSpec(memory_space=pl.ANY)] * 2
        args += list(prev)
        aliases = {6: 0, 7: 1}
    return pl.pallas_call(
        functools.partial(_outproj_kernel, aliased=prev is not None), grid=(m // tm,),
        in_specs=in_specs,
        out_specs=[pl.BlockSpec((tm, D_MODEL), lambda i: (blk_off + i, 0)),
                   pl.BlockSpec((tm * SLABS, LANES), lambda i: (blk_off + i, 0))],
        out_shape=[jax.ShapeDtypeStruct((n_total, D_MODEL), F32),
                   jax.ShapeDtypeStruct((n_total * SLABS, LANES), BF16)],
        scratch_shapes=[pltpu.VMEM((tm * SLABS, LANES), F32)],
        input_output_aliases=aliases,
        compiler_params=_cparams(("arbitrary",)), name="outproj_ln")(*args)


def _router_kernel(h_ref, wr_ref, bias_ref, idx_ref, gate_ref, rank_ref, cnt_ref, run_scr):
    i = pl.program_id(0)
    tt = h_ref.shape[0]

    @pl.when(i == 0)
    def _():
        run_scr[...] = jnp.zeros_like(run_scr)

    neg = -jnp.inf
    logits = lax.dot_general(wr_ref[...], h_ref[...].astype(BF16), (((1,), (1,)), ((), ())),
                             preferred_element_type=F32)
    s = _sigmoid(logits)
    biased = s + bias_ref[...]
    g3 = biased.reshape(N_GROUPS, GROUP_SIZE, tt)
    io_g = lax.broadcasted_iota(I32, (N_GROUPS, GROUP_SIZE, tt), 1).astype(F32)
    m1 = jnp.max(g3, axis=1, keepdims=True)
    f1 = jnp.min(jnp.where(g3 == m1, io_g, float(GROUP_SIZE)), axis=1, keepdims=True)
    m2 = jnp.max(jnp.where(io_g == f1, neg, g3), axis=1, keepdims=True)
    score = (m1 + m2).reshape(N_GROUPS, tt)
    io8 = lax.broadcasted_iota(I32, (N_GROUPS, tt), 0).astype(F32)
    gsel = jnp.zeros((N_GROUPS, tt), F32)
    for _ in range(TOPK_GROUPS):
        m = jnp.max(score, axis=0, keepdims=True)
        f = jnp.min(jnp.where(score == m, io8, float(N_GROUPS)), axis=0, keepdims=True)
        hit = io8 == f
        gsel = jnp.where(hit, 1.0, gsel)
        score = jnp.where(hit, neg, score)
    emask = jnp.broadcast_to(gsel.reshape(N_GROUPS, 1, tt), (N_GROUPS, GROUP_SIZE, tt)).reshape(N_EXPERTS, tt)
    masked = jnp.where(emask > 0.0, biased, neg)
    io_e = lax.broadcasted_iota(I32, (N_EXPERTS, tt), 0).astype(F32)
    hits, idxs, sels = [], [], []
    for _ in range(TOP_K):
        m = jnp.max(masked, axis=0, keepdims=True)
        f = jnp.min(jnp.where(masked == m, io_e, float(N_EXPERTS)), axis=0, keepdims=True)
        hit = io_e == f
        hits.append(hit)
        idxs.append(f)
        sels.append(jnp.sum(jnp.where(hit, s, 0.0), axis=0, keepdims=True))
        masked = jnp.where(hit, neg, masked)
    denom = sels[0]
    for k in range(1, TOP_K):
        denom = denom + sels[k]
    chosen = jnp.zeros((N_EXPERTS, tt), F32)
    for hit in hits:
        chosen = jnp.where(hit, 1.0, chosen)
    ur = lax.broadcasted_iota(I32, (tt, tt), 0)
    uc = lax.broadcasted_iota(I32, (tt, tt), 1)
    upper = (ur < uc).astype(BF16)
    before = run_scr[...] + jnp.dot(chosen.astype(BF16), upper, preferred_element_type=F32)
    ranks = [jnp.sum(jnp.where(hit, before, 0.0), axis=0, keepdims=True) for hit in hits]
    idx_ref[...] = jnp.concatenate(idxs, axis=0).astype(I32)
    gate_ref[...] = jnp.concatenate([sk / denom * ROUTED_SCALE for sk in sels], axis=0)
    rank_ref[...] = jnp.concatenate(ranks, axis=0).astype(I32)
    run = run_scr[...] + jnp.sum(chosen, axis=1, keepdims=True)
    run_scr[...] = run
    cnt_ref[...] = jnp.broadcast_to(run, (N_EXPERTS, LANES)).astype(I32)


def _router(h_all, wr_t_bf, bias_col, tt):
    n = h_all.shape[0]
    tokrow = pl.BlockSpec((TOP_K, tt), lambda i: (0, i))
    return pl.pallas_call(
        _router_kernel, grid=(n // tt,),
        in_specs=[pl.BlockSpec((tt, D_MODEL), lambda i: (i, 0)),
                  pl.BlockSpec((N_EXPERTS, D_MODEL), lambda i: (0, 0)),
                  pl.BlockSpec((N_EXPERTS, 1), lambda i: (0, 0))],
        out_specs=[tokrow, tokrow, tokrow, pl.BlockSpec((N_EXPERTS, LANES), lambda i: (0, 0))],
        out_shape=[jax.ShapeDtypeStruct((TOP_K, n), I32), jax.ShapeDtypeStruct((TOP_K, n), F32),
                   jax.ShapeDtypeStruct((TOP_K, n), I32), jax.ShapeDtypeStruct((N_EXPERTS, LANES), I32)],
        scratch_shapes=[pltpu.VMEM((N_EXPERTS, 1), F32)],
        compiler_params=_cparams(("arbitrary",)), name="router")(h_all, wr_t_bf, bias_col)


def _positions_kernel(idx_ref, rank_ref, pstart_ref, dest_ref):
    tt = idx_ref.shape[1]
    io_e = lax.broadcasted_iota(I32, (N_EXPERTS, tt), 0).astype(F32)
    idx = idx_ref[...].astype(F32)
    first_row = pstart_ref[...].astype(F32)
    rows = [jnp.sum(jnp.where(io_e == idx[k:k + 1, :], first_row, 0.0), axis=0, keepdims=True)
            for k in range(TOP_K)]
    dest_ref[...] = jnp.concatenate(rows, axis=0).astype(I32) + rank_ref[...]


def _positions(idx_t, rank_t, pstart_col, tt):
    n = idx_t.shape[1]
    tok = pl.BlockSpec((TOP_K, tt), lambda i: (0, i))
    return pl.pallas_call(
        _positions_kernel, grid=(n // tt,),
        in_specs=[tok, tok, pl.BlockSpec((N_EXPERTS, 1), lambda i: (0, 0))],
        out_specs=tok, out_shape=jax.ShapeDtypeStruct((TOP_K, n), I32),
        compiler_params=_cparams(("arbitrary",)), name="positions")(idx_t, rank_t, pstart_col)


def _dispatch_kernel(dest_ref, hp_ref, xs_ref, sem):
    tt = dest_ref.shape[1]

    def row_copy(src_row, dst_row):
        return pltpu.make_async_copy(hp_ref.at[pl.ds(pl.multiple_of(src_row * SLABS, SLABS), SLABS), :],
                                     xs_ref.at[pl.ds(pl.multiple_of(dst_row * SLABS, SLABS), SLABS), :], sem)

    def issue(t, carry):
        for k in range(TOP_K):
            row_copy(t, dest_ref[k, t]).start(priority=k % 2)
        return carry

    def drain(t, carry):
        for k in range(TOP_K):
            row_copy(0, 0).wait()
        return carry

    lax.fori_loop(0, tt, issue, 0)
    lax.fori_loop(0, tt, drain, 0)


def _dispatch(dest, hp, n_rows_padded, tt):
    n = dest.shape[1]
    return pl.pallas_call(
        _dispatch_kernel, grid=(n // tt,),
        in_specs=[pl.BlockSpec((TOP_K, tt), lambda i: (0, i), memory_space=pltpu.SMEM),
                  pl.BlockSpec((tt * SLABS, LANES), lambda i: (i, 0))],
        out_specs=pl.BlockSpec(memory_space=pl.ANY),
        out_shape=jax.ShapeDtypeStruct((n_rows_padded * SLABS, LANES), BF16),
        scratch_shapes=[pltpu.SemaphoreType.DMA],
        compiler_params=_cparams(("arbitrary",)), name="dispatch")(dest, hp)


UP_OFF = -(-D_EXPERT // LANES) * LANES


def _experts_kernel(se_ref, sn_ref, nu_ref, send_ref, xs_ref, wg_hbm, wu_hbm, wd_hbm, ys_ref,
                    wg_f, wu_f, wd_f, wgu_s, wd_s, xf_scr, sems, ord_ref):
    b = pl.program_id(0)
    nu = nu_ref[0]
    last = pl.num_programs(0) - 1
    e = se_ref[b]
    live = b < nu
    fresh = (b == 0) | (e != se_ref[jnp.maximum(b - 1, 0)])
    srcs, bufs = (wg_hbm, wu_hbm, wd_hbm), (wg_f, wu_f, wd_f)

    def fetch(j, expert, slot):
        return pltpu.make_async_copy(srcs[j].at[expert], bufs[j].at[slot], sems.at[slot, j])

    def stage(j, slot):
        if j == 0:
            wgu_s[0:D_EXPERT, :] = wg_f[slot].astype(BF16)
        elif j == 1:
            wgu_s[UP_OFF:UP_OFF + D_EXPERT, :] = wu_f[slot].astype(BF16)
        else:
            wd_s[...] = wd_f[slot].astype(BF16)

    nxt1 = send_ref[e]
    e1 = se_ref[jnp.minimum(nxt1, last)]
    nxt2 = jnp.where(nxt1 < nu, send_ref[e1], nu)
    e2 = se_ref[jnp.minimum(nxt2, last)]

    @pl.when(b == 0)
    def _():
        ord_ref[0] = 0
        wgu_s[D_EXPERT:UP_OFF, :] = jnp.zeros((UP_OFF - D_EXPERT, D_MODEL), BF16)
        for j in range(3):
            fetch(j, e, 0).start(priority=WEIGHT_DMA_PRIORITY)

        @pl.when(nxt1 < nu)
        def _():
            for j in range(3):
                fetch(j, e1, 1).start(priority=WEIGHT_DMA_PRIORITY)

    @pl.when(live & fresh)
    def _():
        slot = ord_ref[0] % 2
        for j in range(3):
            fetch(j, e, slot).wait()
            stage(j, slot)

            @pl.when(nxt2 < nu)
            def _():
                fetch(j, e2, slot).start(priority=WEIGHT_DMA_PRIORITY)

        ord_ref[0] = ord_ref[0] + 1

    blk_rows = ROW_BLK * SLABS

    def ffn(n_blk):
        m = n_blk * ROW_BLK
        x = jnp.concatenate([xf_scr[pl.ds(s, m, stride=SLABS), :].astype(BF16) for s in range(SLABS)],
                            axis=1)
        gu = lax.dot_general(x, wgu_s[...], (((1,), (1,)), ((), ())), preferred_element_type=F32)
        g, u = gu[:, :D_EXPERT], gu[:, UP_OFF:UP_OFF + D_EXPERT]
        hm = (g * _sigmoid(g)) * u
        return jnp.dot(hm.astype(BF16), wd_s[...], preferred_element_type=F32)

    def store(n_blk, y):
        for s in range(SLABS):
            ys_ref[pl.ds(s, n_blk * ROW_BLK, stride=SLABS), :] = y[:, s * LANES:(s + 1) * LANES]

    for n_blk in range(1, STEP_BLKS + 1):
        @pl.when(live & (sn_ref[b] == n_blk))
        def _():
            rows = n_blk * blk_rows
            xf_scr[0:rows, :] = xs_ref[0:rows, :].astype(F32)
            store(n_blk, ffn(n_blk))


def _experts(step_expert, step_nblk, n_used, step_end, xs, wg_t, wu_t, wd, n_steps):
    wshape = (D_EXPERT, D_MODEL)
    window = pl.BlockSpec((STEP_ROWS * SLABS, LANES), lambda b, se, sn, nu, sd: (jnp.minimum(b, nu[0] - 1), 0))
    grid_spec = pltpu.PrefetchScalarGridSpec(
        num_scalar_prefetch=4, grid=(n_steps,),
        in_specs=[window] + [pl.BlockSpec(memory_space=pl.ANY)] * 3,
        out_specs=window,
        scratch_shapes=[pltpu.VMEM((2,) + wshape, F32)] * 3
        + [pltpu.VMEM((UP_OFF + D_EXPERT, D_MODEL), BF16), pltpu.VMEM(wshape, BF16),
           pltpu.VMEM((STEP_ROWS * SLABS, LANES), F32), pltpu.SemaphoreType.DMA((2, 3)), pltpu.SMEM((1,), I32)])
    return pl.pallas_call(
        _experts_kernel, grid_spec=grid_spec,
        out_shape=jax.ShapeDtypeStruct((n_steps * STEP_ROWS * SLABS, LANES), F32),
        compiler_params=_cparams(("arbitrary",), vmem_mb=56), name="experts")(
            step_expert, step_nblk, n_used, step_end, xs, wg_t, wu_t, wd)


def _combine_kernel(dest_ref, h_ref, gate_ref, swg_ref, swu_ref, swd_ref, g_ref, b_ref,
                    ys_ref, yp_ref, ysm_ref, gbuf, sem, *, n_prompt_tiles):
    i = pl.program_id(0)
    tt = h_ref.shape[0]

    def row_copy(src_row, dst_row):
        return pltpu.make_async_copy(
            ys_ref.at[pl.ds(pl.multiple_of(src_row * SLABS, SLABS), SLABS), :],
            gbuf.at[pl.ds(pl.multiple_of(dst_row * SLABS, SLABS), SLABS), :], sem)

    def issue(t, carry):
        for k in range(TOP_K):
            row_copy(dest_ref[k, t], k * tt + t).start(priority=k % 2)
        return carry

    def drain(t, carry):
        for k in range(TOP_K):
            row_copy(0, 0).wait()
        return carry

    lax.fori_loop(0, tt, issue, 0)
    h = h_ref[...]
    x = h.astype(BF16)
    nt = (((1,), (1,)), ((), ()))
    gt = lax.dot_general(x, swg_ref[...], nt, preferred_element_type=F32)
    up = lax.dot_general(x, swu_ref[...], nt, preferred_element_type=F32)
    shared = jnp.dot(((gt * _sigmoid(gt)) * up).astype(BF16), swd_ref[...], preferred_element_type=F32)
    lax.fori_loop(0, tt, drain, 0)
    gates = gate_ref[...]
    cols = []
    for s in range(SLABS):
        acc = jnp.zeros((tt, LANES), F32)
        for k in range(TOP_K):
            acc = acc + gates[:, k:k + 1] * gbuf[pl.ds(k * tt * SLABS + s, tt, stride=SLABS), :]
        cols.append(acc)
    routed = jnp.concatenate(cols, axis=1)
    y = _layer_norm(ALPHA * h + (routed + shared), g_ref[...], b_ref[...])

    @pl.when(i < n_prompt_tiles)
    def _():
        yp_ref[...] = y

    @pl.when(i >= n_prompt_tiles)
    def _():
        ysm_ref[...] = y


def _combine(dest, h_all, gates_tok, swg_t, swu_t, swd, g, b, ys, n_prompt, tt):
    n = h_all.shape[0]
    npt = n_prompt // tt
    vec = pl.BlockSpec((1, D_MODEL), lambda i: (0, 0))
    full = lambda a: pl.BlockSpec(a.shape, lambda i: (0, 0))
    return pl.pallas_call(
        functools.partial(_combine_kernel, n_prompt_tiles=npt), grid=(n // tt,),
        in_specs=[pl.BlockSpec((TOP_K, tt), lambda i: (0, i), memory_space=pltpu.SMEM),
                  pl.BlockSpec((tt, D_MODEL), lambda i: (i, 0)),
                  pl.BlockSpec((tt, TOP_K), lambda i: (i, 0)),
                  full(swg_t), full(swu_t), full(swd), vec, vec,
                  pl.BlockSpec(memory_space=pl.ANY)],
        out_specs=[pl.BlockSpec((tt, D_MODEL), lambda i: (jnp.minimum(i, npt - 1), 0)),
                   pl.BlockSpec((tt, D_MODEL), lambda i: (jnp.maximum(i - npt, 0), 0))],
        out_shape=[jax.ShapeDtypeStruct((n_prompt, D_MODEL), F32),
                   jax.ShapeDtypeStruct((n - n_prompt, D_MODEL), F32)],
        scratch_shapes=[pltpu.VMEM((TOP_K * tt * SLABS, LANES), F32), pltpu.SemaphoreType.DMA],
        compiler_params=_cparams(("arbitrary",)), name="combine_shared_ln")(
            dest, h_all, gates_tok, swg_t, swu_t, swd, g, b, ys)


def _mixer(x2d, n_seq, seq_len, s0_bd, shift0, conv0, wts, tm, tn_r, tt_prep, chunk, wkv_nb, tt_conv):
    proj_r = _matmul(x2d, wts["w_r"], tm, tn_r, "inproj_rwkv")
    u = _glu_matmul(x2d, wts["w_ca"], wts["w_cb"], tm, "inproj_glu")
    streams = _rwkv_prep(proj_r, shift0, wts["prep_vecs"], wts["prep_mats"], n_seq, seq_len, tt_prep)
    o_rwkv, s_fin = _wkv(streams, wts["gn_g"], wts["gn_b"], s0_bd, n_seq, seq_len, chunk, wkv_nb)
    conv0_pad = jnp.pad(conv0, ((0, 0), (HALO - (CONV_WIDTH - 1), 0), (0, 0)))
    o_conv = _conv_module(u, conv0_pad, wts["conv_w"], wts["conv_b"], wts["conv_ln_g"], wts["conv_ln_b"],
                          n_seq, seq_len, tt_conv)
    new_shift = proj_r.reshape(n_seq, seq_len, D_SHIFT)[:, seq_len - 1:, :]
    full = jnp.concatenate([conv0, u.reshape(n_seq, seq_len, D_CONV)], axis=1)
    new_conv = full[:, full.shape[1] - (CONV_WIDTH - 1):]
    return o_rwkv, o_conv, _from_block_diag(s_fin), new_shift, new_conv


def kernel(x_prompt, x_sample, state_wkv, state_shift, state_conv, w_in, mu_shift, w0, w2, a0, a2, g2, k_k, k_a, r_k, gn_g, gn_b, conv_w, conv_b, conv_ln_g, conv_ln_b, w_out, ln1_g, ln1_b, w_router, router_bias, exp_w_gate, exp_w_up, exp_w_down, sh_w_gate, sh_w_up, sh_w_down, ln2_g, ln2_b):
    bp, tp, _ = x_prompt.shape
    bs, ts, _ = x_sample.shape
    n_p, n_s = bp * tp, bs * ts
    n_all = n_p + n_s
    row = lambda v: v.reshape(1, -1)
    zpad = jnp.zeros((64, D_RWKV), F32)
    wts = {
        "w_r": w_in[0][:, :D_SHIFT].astype(BF16),
        "w_ca": w_in[0][:, D_SHIFT:D_SHIFT + D_CONV].astype(BF16),
        "w_cb": w_in[0][:, D_SHIFT + D_CONV:].astype(BF16),
        "prep_vecs": [row(mu_shift[0]), row(w0[0]), row(a0[0]), row(k_k[0]), row(k_a[0]), row(r_k[0])],
        "prep_mats": [jnp.concatenate([w2[0], zpad], axis=0).astype(BF16),
                      jnp.concatenate([zpad, a2[0]], axis=0).astype(BF16),
                      g2[0].astype(BF16)],
        "gn_g": row(gn_g[0]), "gn_b": row(gn_b[0]),
        "conv_w": conv_w[0], "conv_b": row(conv_b[0]),
        "conv_ln_g": row(conv_ln_g[0]), "conv_ln_b": row(conv_ln_b[0]),
    }
    w_out_bf = w_out[0].astype(BF16)
    g1, b1 = row(ln1_g[0]), row(ln1_b[0])

    xp = x_prompt.reshape(n_p, D_MODEL)
    zero_s = jnp.zeros((bp, N_PAIRS, LANES, LANES), F32)
    zero_shift = jnp.zeros((bp, 1, D_SHIFT), F32)
    zero_conv = jnp.zeros((bp, CONV_WIDTH - 1, D_CONV), F32)
    orw_p, ocv_p, wkv_p, shift_p, conv_p = _mixer(xp, bp, tp, zero_s, zero_shift, zero_conv, wts,
                                                  tm=512, tn_r=D_SHIFT // 2, tt_prep=256, chunk=64, wkv_nb=bp,
                                                  tt_conv=128)
    xsm = x_sample.reshape(n_s, D_MODEL)
    orw_s, ocv_s, wkv_s, shift_s, conv_s = _mixer(xsm, bs, ts, _to_block_diag(state_wkv[0]), state_shift[0],
                                                  state_conv[0], wts, tm=n_s, tn_r=D_SHIFT // 2, tt_prep=ts,
                                                  chunk=ts, wkv_nb=1, tt_conv=ts)
    h_all, hp_all = _outproj_ln(orw_p, ocv_p, xp, w_out_bf, g1, b1, 256, n_all, 0)
    h_all, hp_all = _outproj_ln(orw_s, ocv_s, xsm, w_out_bf, g1, b1, n_s, n_all, n_p, prev=(h_all, hp_all))

    tt = TOK_TILE
    idx_t, gate_t, rank_t, cnt = _router(h_all, w_router[0].T.astype(BF16), router_bias[0].reshape(-1, 1), tt)
    counts = cnt[:, 0]
    steps = (counts + STEP_ROWS - 1) // STEP_ROWS
    step_end = jnp.cumsum(steps).astype(I32)
    step_start = step_end - steps
    pstart = step_start * STEP_ROWS
    n_steps = _max_steps(n_all * TOP_K)
    n_used = step_end[-1:]
    step = jnp.minimum(jnp.arange(n_steps, dtype=I32), step_end[-1] - 1)
    step_expert = jnp.minimum(jnp.sum((step_end[None, :] <= step[:, None]).astype(I32), axis=1), N_EXPERTS - 1)
    mine = step_expert[:, None] == jnp.arange(N_EXPERTS, dtype=I32)[None, :]
    rows_left = jnp.sum(jnp.where(mine, counts[None, :] - (step[:, None] - step_start[None, :]) * STEP_ROWS, 0),
                        axis=1)
    step_nblk = jnp.clip((rows_left + ROW_BLK - 1) // ROW_BLK, 1, STEP_BLKS).astype(I32)
    dest = _positions(idx_t, rank_t, pstart.reshape(-1, 1), 5 * TOK_TILE)
    xs = _dispatch(dest, hp_all, n_steps * STEP_ROWS, tt)
    ys = _experts(step_expert, step_nblk, n_used, step_end, xs, jnp.swapaxes(exp_w_gate[0], 1, 2),
                  jnp.swapaxes(exp_w_up[0], 1, 2), exp_w_down[0], n_steps)
    y_p, y_s = _combine(dest, h_all, gate_t.T, sh_w_gate[0].T.astype(BF16),
                        sh_w_up[0].T.astype(BF16), sh_w_down[0].astype(BF16), row(ln2_g[0]), row(ln2_b[0]),
                        ys, n_p, tt)
    return (y_p.reshape(bp, tp, D_MODEL), y_s.reshape(bs, ts, D_MODEL),
            wkv_p[None], shift_p[None], conv_p[None], wkv_s[None], shift_s[None], conv_s[None])
```

```python
import functools
import math

import jax
import jax.numpy as jnp
from jax import lax
from jax.experimental import pallas as pl
from jax.experimental.pallas import tpu as pltpu

F32 = jnp.float32
BF16 = jnp.bfloat16
I32 = jnp.int32
HIGHEST = lax.Precision.HIGHEST

D_MODEL = 2048
D_RWKV = 1024
D_CONV = 1024
HEAD_DIM = 64
N_HEADS = D_RWKV // HEAD_DIM
N_PAIRS = N_HEADS // 2
CONV_WIDTH = 31
D_SHIFT = 3 * D_RWKV + 64 + 64 + 128
N_EXPERTS = 256
TOP_K = 8
N_GROUPS = 8
GROUP_SIZE = N_EXPERTS // N_GROUPS
TOPK_GROUPS = 4
D_EXPERT = 576
ROUTED_SCALE = 2.5
ALPHA = 2.0 ** 0.25
LN_EPS = 1e-5
GN_EPS = 64e-5

LANES = 128
ROW_BLK = 128
STEP_BLKS = 3
STEP_ROWS = STEP_BLKS * ROW_BLK
TOK_TILE = 128
SLABS = D_MODEL // LANES
WEIGHT_DMA_PRIORITY = 1


def _max_steps(n_rows):
    return -(-(n_rows + N_EXPERTS * (STEP_ROWS - 1)) // STEP_ROWS)


def _cparams(sem, vmem_mb=48):
    return pltpu.CompilerParams(dimension_semantics=sem, vmem_limit_bytes=vmem_mb * 2 ** 20)


def _sigmoid(x):
    return 1.0 / (1.0 + jnp.exp(-x))


def _dot(a, b):
    return jnp.dot(a.astype(BF16), b.astype(BF16), preferred_element_type=F32)


def _dot_nt(a, b):
    return lax.dot_general(a.astype(BF16), b.astype(BF16), (((1,), (1,)), ((), ())),
                           preferred_element_type=F32)


def _dot_tn(a, b):
    return lax.dot_general(a.astype(BF16), b.astype(BF16), (((0,), (0,)), ((), ())),
                           preferred_element_type=F32)


def _head_ones():
    r = lax.broadcasted_iota(I32, (LANES, LANES), 0) // HEAD_DIM
    c = lax.broadcasted_iota(I32, (LANES, LANES), 1) // HEAD_DIM
    return (r == c).astype(F32)


def _segsum(x, ones_bd):
    outs = [jnp.dot(x[:, s * LANES:(s + 1) * LANES], ones_bd, precision=HIGHEST,
                    preferred_element_type=F32) for s in range(x.shape[1] // LANES)]
    return outs[0] if len(outs) == 1 else jnp.concatenate(outs, axis=1)


def _layer_norm(x, g, b):
    mu = jnp.mean(x, axis=-1, keepdims=True)
    d = x - mu
    var = jnp.mean(d * d, axis=-1, keepdims=True)
    return d * lax.rsqrt(var + LN_EPS) * g + b


def _mm_kernel(x_ref, w_ref, o_ref):
    o_ref[...] = jnp.dot(x_ref[...].astype(BF16), w_ref[...], preferred_element_type=F32)


def _matmul(x, w_bf, tm, tn, name):
    m, k = x.shape
    n = w_bf.shape[1]
    return pl.pallas_call(
        _mm_kernel, grid=(n // tn, m // tm),
        in_specs=[pl.BlockSpec((tm, k), lambda j, i: (i, 0)),
                  pl.BlockSpec((k, tn), lambda j, i: (0, j))],
        out_specs=pl.BlockSpec((tm, tn), lambda j, i: (i, j)),
        out_shape=jax.ShapeDtypeStruct((m, n), F32),
        compiler_params=_cparams(("arbitrary", "arbitrary")), name=name)(x, w_bf)


def _glu_kernel(x_ref, wa_ref, wb_ref, o_ref):
    x = x_ref[...].astype(BF16)
    a = jnp.dot(x, wa_ref[...], preferred_element_type=F32)
    b = jnp.dot(x, wb_ref[...], preferred_element_type=F32)
    o_ref[...] = a * _sigmoid(b)


def _glu_matmul(x, wa_bf, wb_bf, tm, name):
    m, k = x.shape
    n = wa_bf.shape[1]
    return pl.pallas_call(
        _glu_kernel, grid=(m // tm,),
        in_specs=[pl.BlockSpec((tm, k), lambda i: (i, 0)),
                  pl.BlockSpec((k, n), lambda i: (0, 0)),
                  pl.BlockSpec((k, n), lambda i: (0, 0))],
        out_specs=pl.BlockSpec((tm, n), lambda i: (i, 0)),
        out_shape=jax.ShapeDtypeStruct((m, n), F32),
        compiler_params=_cparams(("arbitrary",)), name=name)(x, wa_bf, wb_bf)


def _prep_kernel(p_ref, pv_ref, sh_ref, mu_ref, w0_ref, a0_ref, kk_ref, ka_ref, rk_ref,
                 w2_ref, a2_ref, g2_ref,
                 r_out, lw_out, k_out, v_out, kn_out, b_out, g_out, bonus_out):
    i = pl.program_id(1)
    p = p_ref[...]
    tt = p.shape[0]
    carry = jnp.where(i == 0, sh_ref[...], pv_ref[7:8, :])
    row = lax.broadcasted_iota(I32, (tt, 1), 0)
    prev = jnp.where(row == 0, carry, pltpu.roll(p, 1, 0))
    xs = p + (prev - p) * mu_ref[...]
    r = xs[:, 0:D_RWKV]
    k = xs[:, D_RWKV:2 * D_RWKV]
    v = xs[:, 2 * D_RWKV:3 * D_RWKV]
    wa = xs[:, 3 * D_RWKV:3 * D_RWKV + 128]
    gd = xs[:, 3 * D_RWKV + 128:]
    z = w0_ref[...] + _dot(jnp.tanh(wa), w2_ref[...])
    w_log = -(jnp.maximum(-z, 0.0) + jnp.log1p(jnp.exp(-jnp.abs(z)))) - 0.5
    lw_out[...] = -jnp.exp(w_log)
    a = _sigmoid(a0_ref[...] + _dot(wa, a2_ref[...]))
    g_out[...] = _dot(_sigmoid(gd), g2_ref[...])
    ones_bd = _head_ones()
    kk = k * kk_ref[...]
    nrm = jnp.sqrt(_segsum(kk * kk, ones_bd))
    kn = kk / jnp.maximum(nrm, 1e-12)
    kh = k * (1.0 + (a - 1.0) * ka_ref[...])
    r_out[...] = r
    k_out[...] = kh
    v_out[...] = v
    kn_out[...] = kn
    b_out[...] = kn * a
    bonus_out[...] = _segsum(r * kh * rk_ref[...], ones_bd) * v


def _rwkv_prep(proj, shift0, vecs, mats, n_seq, seq_len, tt):
    nt = seq_len // tt
    n = n_seq * seq_len
    row_vec = lambda c: pl.BlockSpec((1, c), lambda b, i: (0, 0))
    full = lambda a: pl.BlockSpec(a.shape, lambda b, i: (0, 0))
    out_spec = pl.BlockSpec((tt, D_RWKV), lambda b, i: (b * nt + i, 0))
    outs = pl.pallas_call(
        _prep_kernel, grid=(n_seq, nt),
        in_specs=[pl.BlockSpec((tt, D_SHIFT), lambda b, i: (b * nt + i, 0)),
                  pl.BlockSpec((8, D_SHIFT), lambda b, i: (jnp.maximum((b * nt + i) * (tt // 8) - 1, 0), 0)),
                  pl.BlockSpec((None, 1, D_SHIFT), lambda b, i: (b, 0, 0)),
                  row_vec(D_SHIFT)] + [row_vec(D_RWKV)] * 5 + [full(m) for m in mats],
        out_specs=[out_spec] * 8,
        out_shape=[jax.ShapeDtypeStruct((n, D_RWKV), F32)] * 8,
        compiler_params=_cparams(("arbitrary", "arbitrary")), name="rwkv_prep")(
            proj, proj, shift0, *vecs, *mats)
    return outs


def _wkv_kernel(r_ref, lw_ref, k_ref, v_ref, kn_ref, b_ref, g_ref, bonus_ref, gng_ref, gnb_ref, s0_ref,
                o_ref, sfin_ref, s_scr):
    c = pl.program_id(1)
    nc = pl.num_programs(1)
    nb, C = r_ref.shape[0], r_ref.shape[1]
    C2 = 2 * C
    fused = C2 == LANES
    pairs = range(nb * N_PAIRS)

    @pl.when(c == 0)
    def _():
        s_scr[...] = s0_ref[...]

    lane = lax.broadcasted_iota(I32, (C, LANES), 1)
    m0 = lane < HEAD_DIM
    ri = lax.broadcasted_iota(I32, (C2, C2), 0)
    ci = lax.broadcasted_iota(I32, (C2, C2), 1)
    same = (ri // C) == (ci // C)
    strict = same & (ri > ci)
    incl = same & (ri >= ci)
    eye = (ri == ci).astype(F32)
    tr = lax.broadcasted_iota(I32, (C, C), 0)
    tc = lax.broadcasted_iota(I32, (C, C), 1)
    ltri = (tr >= tc).astype(BF16)
    ones_bd = _head_ones().astype(BF16)
    n_sq = int(math.log2(C)) - 1

    def stack(x):
        return jnp.concatenate([jnp.where(m0, x, 0.0), jnp.where(m0, 0.0, x)], axis=0).astype(BF16)

    def split_dot(lhs_bf, x, rhs_bf, terms):
        acc, rem = None, x
        for _ in range(terms):
            piece = rem.astype(BF16)
            part = (jnp.dot(lhs_bf, piece, preferred_element_type=F32) if rhs_bf is None
                    else jnp.dot(piece, rhs_bf, preferred_element_type=F32))
            acc = part if acc is None else acc + part
            rem = rem - piece.astype(F32)
        return acc

    lw_all = [lw_ref[q] for q in range(nb)]
    lc_all = [split_dot(ltri, lw, None, 3) for lw in lw_all]
    sls = [slice((p % N_PAIRS) * LANES, (p % N_PAIRS + 1) * LANES) for p in pairs]
    seq = [p // N_PAIRS for p in pairs]

    ar, bk, vst, endst, decay_end = [], [], [], [], []
    for p in pairs:
        sl, q = sls[p], seq[p]
        lw, lc = lw_all[q][:, sl], lc_all[q][:, sl]
        lc_end = lc[C - 1:C, :]
        e_neg = jnp.exp(-lc)
        e_end = jnp.exp(lc_end - lc)
        bb, kh = b_ref[q, :, sl], k_ref[q, :, sl]
        ar.append(jnp.concatenate([stack(-kn_ref[q, :, sl] * jnp.exp(lc - lw)),
                                   stack(r_ref[q, :, sl] * jnp.exp(lc))], axis=0))
        bk.append(jnp.concatenate([stack(bb * e_neg), stack(kh * e_neg)], axis=0))
        vst.append(stack(v_ref[q, :, sl]))
        endst.append(jnp.concatenate([stack(bb * e_end), stack(kh * e_end)], axis=0))
        decay_end.append(jnp.exp(lc_end))

    ab, ak, rbk = [], [], []
    for p in pairs:
        if fused:
            m = _dot_nt(ar[p], bk[p])
            ab.append(jnp.where(strict, m[:C2, :C2], 0.0))
            ak.append(jnp.where(strict, m[:C2, C2:], 0.0).astype(BF16))
            rbk.append(jnp.concatenate([jnp.where(incl, m[C2:, :C2], 0.0),
                                        jnp.where(incl, m[C2:, C2:], 0.0)], axis=1).astype(BF16))
        else:
            a_st, r_st, b_st, k_st = ar[p][:C2], ar[p][C2:], bk[p][:C2], bk[p][C2:]
            ab.append(jnp.where(strict, _dot_nt(a_st, b_st), 0.0))
            ak.append(jnp.where(strict, _dot_nt(a_st, k_st), 0.0).astype(BF16))
            rbk.append((jnp.where(incl, _dot_nt(r_st, b_st), 0.0).astype(BF16),
                        jnp.where(incl, _dot_nt(r_st, k_st), 0.0).astype(BF16)))

    tm = [eye + ab[p] for p in pairs]
    pw = [_dot(ab[p], ab[p]) for p in pairs]
    for _ in range(n_sq - 1):
        for p in pairs:
            both = _dot(jnp.concatenate([tm[p], pw[p]], axis=0), pw[p])
            tm[p] = tm[p] + both[:C2]
            pw[p] = both[C2:]
    for p in pairs:
        tm[p] = (tm[p] + _dot(tm[p], pw[p])).astype(BF16)

    s_old = [s_scr[seq[p], p % N_PAIRS] for p in pairs]
    ars = [_dot_nt(ar[p], s_old[p]) for p in pairs]
    rhs = [ars[p][:C2] + jnp.dot(ak[p], vst[p], preferred_element_type=F32) for p in pairs]
    uv = [jnp.concatenate([jnp.dot(tm[p], rhs[p].astype(BF16), preferred_element_type=F32).astype(BF16),
                           vst[p]], axis=0) for p in pairs]
    for p in pairs:
        s_scr[seq[p], p % N_PAIRS] = s_old[p] * decay_end[p] + _dot_tn(uv[p], endst[p])
    for p in pairs:
        sl, q = sls[p], seq[p]
        if fused:
            o_st = ars[p][C2:] + jnp.dot(rbk[p], uv[p], preferred_element_type=F32)
        else:
            o_st = (ars[p][C2:] + jnp.dot(rbk[p][0], uv[p][:C2], preferred_element_type=F32)
                    + jnp.dot(rbk[p][1], uv[p][C2:], preferred_element_type=F32))
        o = o_st[:C] + o_st[C:]
        mu = split_dot(None, o, ones_bd, 2) * (1.0 / HEAD_DIM)
        d = o - mu
        var = split_dot(None, d * d, ones_bd, 2) * (1.0 / HEAD_DIM)
        on = d * lax.rsqrt(var + GN_EPS) * gng_ref[:, sl] + gnb_ref[:, sl]
        o_ref[q, :, sl] = (on + bonus_ref[q, :, sl]) * g_ref[q, :, sl]

    @pl.when(c == nc - 1)
    def _():
        sfin_ref[...] = s_scr[...]


def _wkv(streams, gn_g, gn_b, s0_bd, n_seq, seq_len, chunk, nb):
    nc = seq_len // chunk
    tok = pl.BlockSpec((nb, chunk, D_RWKV), lambda b, c: (b, c, 0))
    vec = pl.BlockSpec((1, D_RWKV), lambda b, c: (0, 0))
    st = pl.BlockSpec((nb, N_PAIRS, LANES, LANES), lambda b, c: (b, 0, 0, 0))
    o, s_fin = pl.pallas_call(
        _wkv_kernel, grid=(n_seq // nb, nc),
        in_specs=[tok] * 8 + [vec, vec, st],
        out_specs=[tok, st],
        out_shape=[jax.ShapeDtypeStruct((n_seq, seq_len, D_RWKV), F32),
                   jax.ShapeDtypeStruct((n_seq, N_PAIRS, LANES, LANES), F32)],
        scratch_shapes=[pltpu.VMEM((nb, N_PAIRS, LANES, LANES), F32)],
        compiler_params=_cparams(("arbitrary", "arbitrary")), name="wkv_chunks")(
            *[s.reshape(n_seq, seq_len, D_RWKV) for s in streams], gn_g, gn_b, s0_bd)
    return o.reshape(n_seq * seq_len, D_RWKV), s_fin


def _to_block_diag(s):
    b = s.shape[0]
    s = s.reshape(b, N_PAIRS, 2, HEAD_DIM, HEAD_DIM)
    z = jnp.zeros_like(s[:, :, 0])
    top = jnp.concatenate([s[:, :, 0], z], axis=-1)
    bot = jnp.concatenate([z, s[:, :, 1]], axis=-1)
    return jnp.concatenate([top, bot], axis=-2)


def _from_block_diag(bd):
    b = bd.shape[0]
    h0 = bd[:, :, :HEAD_DIM, :HEAD_DIM]
    h1 = bd[:, :, HEAD_DIM:, HEAD_DIM:]
    return jnp.stack([h0, h1], axis=2).reshape(b, N_HEADS, HEAD_DIM, HEAD_DIM)


HALO = 32


def _conv_kernel(u_ref, halo_ref, cp_ref, cw_ref, cb_ref, lg_ref, lb_ref, o_ref, xbuf, ybuf, wbuf):
    i = pl.program_id(1)
    tt = u_ref.shape[0]
    xbuf[0:HALO, :] = jnp.where(i == 0, cp_ref[...], halo_ref[...])
    xbuf[HALO:HALO + tt, :] = u_ref[...]
    for cs in range(D_CONV // LANES):
        sl = slice(cs * LANES, (cs + 1) * LANES)
        for r in range(8):
            wbuf[r] = xbuf[8 - r:8 - r + tt + 24, sl]
        acc = jnp.zeros((tt, LANES), F32)
        for r in range(8):
            for q in range(4):
                s = 8 * q + r
                if s > CONV_WIDTH - 1:
                    continue
                j = CONV_WIDTH - 1 - s
                acc = acc + cw_ref[j:j + 1, sl] * wbuf[r, 24 - 8 * q:24 - 8 * q + tt, :]
        ybuf[:, sl] = acc + cb_ref[:, sl]
    y = _layer_norm(ybuf[...], lg_ref[...], lb_ref[...])
    o_ref[...] = y * _sigmoid(y)


def _conv_module(u, conv0_pad, cw, cb, lg, lb, n_seq, seq_len, tt):
    nt = seq_len // tt
    n = n_seq * seq_len
    vec = pl.BlockSpec((1, D_CONV), lambda b, i: (0, 0))
    return pl.pallas_call(
        _conv_kernel, grid=(n_seq, nt),
        in_specs=[pl.BlockSpec((tt, D_CONV), lambda b, i: (b * nt + i, 0)),
                  pl.BlockSpec((HALO, D_CONV),
                               lambda b, i: (jnp.maximum((b * seq_len + i * tt) // HALO - 1, 0), 0)),
                  pl.BlockSpec((None, HALO, D_CONV), lambda b, i: (b, 0, 0)),
                  pl.BlockSpec((CONV_WIDTH, D_CONV), lambda b, i: (0, 0)), vec, vec, vec],
        out_specs=pl.BlockSpec((tt, D_CONV), lambda b, i: (b * nt + i, 0)),
        out_shape=jax.ShapeDtypeStruct((n, D_CONV), F32),
        scratch_shapes=[pltpu.VMEM((HALO + tt, D_CONV), F32), pltpu.VMEM((tt, D_CONV), F32),
                        pltpu.VMEM((8, tt + 24, LANES), F32)],
        compiler_params=_cparams(("arbitrary", "arbitrary")), name="conv_module")(
            u, u, conv0_pad, cw, cb, lg, lb)


def _outproj_kernel(*refs, aliased):
    if aliased:
        orw_ref, ocv_ref, x_ref, w_ref, g_ref, b_ref, _, _, h_ref, hp_ref, tile_scr = refs
    else:
        orw_ref, ocv_ref, x_ref, w_ref, g_ref, b_ref, h_ref, hp_ref, tile_scr = refs
    tm = x_ref.shape[0]
    mix = (jnp.dot(orw_ref[...].astype(BF16), w_ref[0:D_RWKV, :], preferred_element_type=F32)
           + jnp.dot(ocv_ref[...].astype(BF16), w_ref[D_RWKV:, :], preferred_element_type=F32))
    h = _layer_norm(ALPHA * x_ref[...] + mix, g_ref[...], b_ref[...])
    h_ref[...] = h
    for s in range(SLABS):
        tile_scr[pl.ds(s, tm, stride=SLABS), :] = h[:, s * LANES:(s + 1) * LANES]
    hp_ref[...] = tile_scr[...].astype(BF16)


def _outproj_ln(o_rwkv, o_conv, x, w_bf, g, b, tm, n_total, row_off, prev=None):
    m = x.shape[0]
    blk_off = row_off // tm
    vec = pl.BlockSpec((1, D_MODEL), lambda i: (0, 0))
    in_specs = [pl.BlockSpec((tm, D_RWKV), lambda i: (i, 0)),
                pl.BlockSpec((tm, D_CONV), lambda i: (i, 0)),
                pl.BlockSpec((tm, D_MODEL), lambda i: (i, 0)),
                pl.BlockSpec((D_MODEL, D_MODEL), lambda i: (0, 0)), vec, vec]
    args = [o_rwkv, o_conv, x, w_bf, g, b]
    aliases = {}
    if prev is not None:
        in_specs += [pl.BlockSpec(memory_space=pl.ANY)] * 2
        args += list(prev)
        aliases = {6: 0, 7: 1}
    return pl.pallas_call(
        functools.partial(_outproj_kernel, aliased=prev is not None), grid=(m // tm,),
        in_specs=in_specs,
        out_specs=[pl.BlockSpec((tm, D_MODEL), lambda i: (blk_off + i, 0)),
                   pl.BlockSpec((tm * SLABS, LANES), lambda i: (blk_off + i, 0))],
        out_shape=[jax.ShapeDtypeStruct((n_total, D_MODEL), F32),
                   jax.ShapeDtypeStruct((n_total * SLABS, LANES), BF16)],
        scratch_shapes=[pltpu.VMEM((tm * SLABS, LANES), F32)],
        input_output_aliases=aliases,
        compiler_params=_cparams(("arbitrary",)), name="outproj_ln")(*args)


def _router_kernel(h_ref, wr_ref, bias_ref, idx_ref, gate_ref, rank_ref, cnt_ref, run_scr):
    i = pl.program_id(0)
    tt = h_ref.shape[0]

    @pl.when(i == 0)
    def _():
        run_scr[...] = jnp.zeros_like(run_scr)

    neg = -jnp.inf
    logits = lax.dot_general(wr_ref[...], h_ref[...].astype(BF16), (((1,), (1,)), ((), ())),
                             preferred_element_type=F32)
    s = _sigmoid(logits)
    biased = s + bias_ref[...]
    g3 = biased.reshape(N_GROUPS, GROUP_SIZE, tt)
    io_g = lax.broadcasted_iota(I32, (N_GROUPS, GROUP_SIZE, tt), 1).astype(F32)
    m1 = jnp.max(g3, axis=1, keepdims=True)
    f1 = jnp.min(jnp.where(g3 == m1, io_g, float(GROUP_SIZE)), axis=1, keepdims=True)
    m2 = jnp.max(jnp.where(io_g == f1, neg, g3), axis=1, keepdims=True)
    score = (m1 + m2).reshape(N_GROUPS, tt)
    io8 = lax.broadcasted_iota(I32, (N_GROUPS, tt), 0).astype(F32)
    gsel = jnp.zeros((N_GROUPS, tt), F32)
    for _ in range(TOPK_GROUPS):
        m = jnp.max(score, axis=0, keepdims=True)
        f = jnp.min(jnp.where(score == m, io8, float(N_GROUPS)), axis=0, keepdims=True)
        hit = io8 == f
        gsel = jnp.where(hit, 1.0, gsel)
        score = jnp.where(hit, neg, score)
    emask = jnp.broadcast_to(gsel.reshape(N_GROUPS, 1, tt), (N_GROUPS, GROUP_SIZE, tt)).reshape(N_EXPERTS, tt)
    masked = jnp.where(emask > 0.0, biased, neg)
    io_e = lax.broadcasted_iota(I32, (N_EXPERTS, tt), 0).astype(F32)
    hits, idxs, sels = [], [], []
    for _ in range(TOP_K):
        m = jnp.max(masked, axis=0, keepdims=True)
        f = jnp.min(jnp.where(masked == m, io_e, float(N_EXPERTS)), axis=0, keepdims=True)
        hit = io_e == f
        hits.append(hit)
        idxs.append(f)
        sels.append(jnp.sum(jnp.where(hit, s, 0.0), axis=0, keepdims=True))
        masked = jnp.where(hit, neg, masked)
    denom = sels[0]
    for k in range(1, TOP_K):
        denom = denom + sels[k]
    chosen = jnp.zeros((N_EXPERTS, tt), F32)
    for hit in hits:
        chosen = jnp.where(hit, 1.0, chosen)
    ur = lax.broadcasted_iota(I32, (tt, tt), 0)
    uc = lax.broadcasted_iota(I32, (tt, tt), 1)
    upper = (ur < uc).astype(BF16)
    before = run_scr[...] + jnp.dot(chosen.astype(BF16), upper, preferred_element_type=F32)
    ranks = [jnp.sum(jnp.where(hit, before, 0.0), axis=0, keepdims=True) for hit in hits]
    idx_ref[...] = jnp.concatenate(idxs, axis=0).astype(I32)
    gate_ref[...] = jnp.concatenate([sk / denom * ROUTED_SCALE for sk in sels], axis=0)
    rank_ref[...] = jnp.concatenate(ranks, axis=0).astype(I32)
    run = run_scr[...] + jnp.sum(chosen, axis=1, keepdims=True)
    run_scr[...] = run
    cnt_ref[...] = jnp.broadcast_to(run, (N_EXPERTS, LANES)).astype(I32)


def _router(h_all, wr_t_bf, bias_col, tt):
    n = h_all.shape[0]
    tokrow = pl.BlockSpec((TOP_K, tt), lambda i: (0, i))
    return pl.pallas_call(
        _router_kernel, grid=(n // tt,),
        in_specs=[pl.BlockSpec((tt, D_MODEL), lambda i: (i, 0)),
                  pl.BlockSpec((N_EXPERTS, D_MODEL), lambda i: (0, 0)),
                  pl.BlockSpec((N_EXPERTS, 1), lambda i: (0, 0))],
        out_specs=[tokrow, tokrow, tokrow, pl.BlockSpec((N_EXPERTS, LANES), lambda i: (0, 0))],
        out_shape=[jax.ShapeDtypeStruct((TOP_K, n), I32), jax.ShapeDtypeStruct((TOP_K, n), F32),
                   jax.ShapeDtypeStruct((TOP_K, n), I32), jax.ShapeDtypeStruct((N_EXPERTS, LANES), I32)],
        scratch_shapes=[pltpu.VMEM((N_EXPERTS, 1), F32)],
        compiler_params=_cparams(("arbitrary",)), name="router")(h_all, wr_t_bf, bias_col)


def _positions_kernel(idx_ref, rank_ref, pstart_ref, dest_ref):
    tt = idx_ref.shape[1]
    io_e = lax.broadcasted_iota(I32, (N_EXPERTS, tt), 0).astype(F32)
    idx = idx_ref[...].astype(F32)
    first_row = pstart_ref[...].astype(F32)
    rows = [jnp.sum(jnp.where(io_e == idx[k:k + 1, :], first_row, 0.0), axis=0, keepdims=True)
            for k in range(TOP_K)]
    dest_ref[...] = jnp.concatenate(rows, axis=0).astype(I32) + rank_ref[...]


def _positions(idx_t, rank_t, pstart_col, tt):
    n = idx_t.shape[1]
    tok = pl.BlockSpec((TOP_K, tt), lambda i: (0, i))
    return pl.pallas_call(
        _positions_kernel, grid=(n // tt,),
        in_specs=[tok, tok, pl.BlockSpec((N_EXPERTS, 1), lambda i: (0, 0))],
        out_specs=tok, out_shape=jax.ShapeDtypeStruct((TOP_K, n), I32),
        compiler_params=_cparams(("arbitrary",)), name="positions")(idx_t, rank_t, pstart_col)


def _dispatch_kernel(dest_ref, hp_ref, xs_ref, sem):
    tt = dest_ref.shape[1]

    def row_copy(src_row, dst_row):
        return pltpu.make_async_copy(hp_ref.at[pl.ds(pl.multiple_of(src_row * SLABS, SLABS), SLABS), :],
                                     xs_ref.at[pl.ds(pl.multiple_of(dst_row * SLABS, SLABS), SLABS), :], sem)

    def issue(t, carry):
        for k in range(TOP_K):
            row_copy(t, dest_ref[k, t]).start(priority=k % 2)
        return carry

    def drain(t, carry):
        for k in range(TOP_K):
            row_copy(0, 0).wait()
        return carry

    lax.fori_loop(0, tt, issue, 0)
    lax.fori_loop(0, tt, drain, 0)


def _dispatch(dest, hp, n_rows_padded, tt):
    n = dest.shape[1]
    return pl.pallas_call(
        _dispatch_kernel, grid=(n // tt,),
        in_specs=[pl.BlockSpec((TOP_K, tt), lambda i: (0, i), memory_space=pltpu.SMEM),
                  pl.BlockSpec((tt * SLABS, LANES), lambda i: (i, 0))],
        out_specs=pl.BlockSpec(memory_space=pl.ANY),
        out_shape=jax.ShapeDtypeStruct((n_rows_padded * SLABS, LANES), BF16),
        scratch_shapes=[pltpu.SemaphoreType.DMA],
        compiler_params=_cparams(("arbitrary",)), name="dispatch")(dest, hp)


UP_OFF = -(-D_EXPERT // LANES) * LANES


def _experts_kernel(se_ref, sn_ref, nu_ref, send_ref, xs_ref, wg_hbm, wu_hbm, wd_hbm, ys_ref,
                    wg_f, wu_f, wd_f, wgu_s, wd_s, xf_scr, sems, ord_ref):
    b = pl.program_id(0)
    nu = nu_ref[0]
    last = pl.num_programs(0) - 1
    e = se_ref[b]
    live = b < nu
    fresh = (b == 0) | (e != se_ref[jnp.maximum(b - 1, 0)])
    srcs, bufs = (wg_hbm, wu_hbm, wd_hbm), (wg_f, wu_f, wd_f)

    def fetch(j, expert, slot):
        return pltpu.make_async_copy(srcs[j].at[expert], bufs[j].at[slot], sems.at[slot, j])

    def stage(j, slot):
        if j == 0:
            wgu_s[0:D_EXPERT, :] = wg_f[slot].astype(BF16)
        elif j == 1:
            wgu_s[UP_OFF:UP_OFF + D_EXPERT, :] = wu_f[slot].astype(BF16)
        else:
            wd_s[...] = wd_f[slot].astype(BF16)

    nxt1 = send_ref[e]
    e1 = se_ref[jnp.minimum(nxt1, last)]
    nxt2 = jnp.where(nxt1 < nu, send_ref[e1], nu)
    e2 = se_ref[jnp.minimum(nxt2, last)]

    @pl.when(b == 0)
    def _():
        ord_ref[0] = 0
        wgu_s[D_EXPERT:UP_OFF, :] = jnp.zeros((UP_OFF - D_EXPERT, D_MODEL), BF16)
        for j in range(3):
            fetch(j, e, 0).start(priority=WEIGHT_DMA_PRIORITY)

        @pl.when(nxt1 < nu)
        def _():
            for j in range(3):
                fetch(j, e1, 1).start(priority=WEIGHT_DMA_PRIORITY)

    @pl.when(live & fresh)
    def _():
        slot = ord_ref[0] % 2
        for j in range(3):
            fetch(j, e, slot).wait()
            stage(j, slot)

            @pl.when(nxt2 < nu)
            def _():
                fetch(j, e2, slot).start(priority=WEIGHT_DMA_PRIORITY)

        ord_ref[0] = ord_ref[0] + 1

    blk_rows = ROW_BLK * SLABS

    def ffn(n_blk):
        m = n_blk * ROW_BLK
        x = jnp.concatenate([xf_scr[pl.ds(s, m, stride=SLABS), :].astype(BF16) for s in range(SLABS)],
                            axis=1)
        gu = lax.dot_general(x, wgu_s[...], (((1,), (1,)), ((), ())), preferred_element_type=F32)
        g, u = gu[:, :D_EXPERT], gu[:, UP_OFF:UP_OFF + D_EXPERT]
        hm = (g * _sigmoid(g)) * u
        return jnp.dot(hm.astype(BF16), wd_s[...], preferred_element_type=F32)

    def store(n_blk, y):
        rows = n_blk * blk_rows
        for s in range(SLABS):
            xf_scr[pl.ds(s, n_blk * ROW_BLK, stride=SLABS), :] = y[:, s * LANES:(s + 1) * LANES]
        ys_ref[0:rows, :] = xf_scr[0:rows, :].astype(BF16)

    for n_blk in range(1, STEP_BLKS + 1):
        @pl.when(live & (sn_ref[b] == n_blk))
        def _():
            rows = n_blk * blk_rows
            xf_scr[0:rows, :] = xs_ref[0:rows, :].astype(F32)
            store(n_blk, ffn(n_blk))


def _experts(step_expert, step_nblk, n_used, step_end, xs, wg_t, wu_t, wd, n_steps):
    wshape = (D_EXPERT, D_MODEL)
    window = pl.BlockSpec((STEP_ROWS * SLABS, LANES), lambda b, se, sn, nu, sd: (jnp.minimum(b, nu[0] - 1), 0))
    grid_spec = pltpu.PrefetchScalarGridSpec(
        num_scalar_prefetch=4, grid=(n_steps,),
        in_specs=[window] + [pl.BlockSpec(memory_space=pl.ANY)] * 3,
        out_specs=window,
        scratch_shapes=[pltpu.VMEM((2,) + wshape, F32)] * 3
        + [pltpu.VMEM((UP_OFF + D_EXPERT, D_MODEL), BF16), pltpu.VMEM(wshape, BF16),
           pltpu.VMEM((STEP_ROWS * SLABS, LANES), F32), pltpu.SemaphoreType.DMA((2, 3)), pltpu.SMEM((1,), I32)])
    return pl.pallas_call(
        _experts_kernel, grid_spec=grid_spec,
        out_shape=jax.ShapeDtypeStruct((n_steps * STEP_ROWS * SLABS, LANES), BF16),
        compiler_params=_cparams(("arbitrary",), vmem_mb=56), name="experts")(
            step_expert, step_nblk, n_used, step_end, xs, wg_t, wu_t, wd)


def _combine_kernel(dest_ref, h_ref, gate_ref, swg_ref, swu_ref, swd_ref, g_ref, b_ref,
                    ys_ref, yp_ref, ysm_ref, gbuf, gf32, sem, *, n_prompt_tiles):
    i = pl.program_id(0)
    tt = h_ref.shape[0]

    def row_copy(src_row, dst_row):
        return pltpu.make_async_copy(
            ys_ref.at[pl.ds(pl.multiple_of(src_row * SLABS, SLABS), SLABS), :],
            gbuf.at[pl.ds(pl.multiple_of(dst_row * SLABS, SLABS), SLABS), :], sem)

    def issue(t, carry):
        for k in range(TOP_K):
            row_copy(dest_ref[k, t], k * tt + t).start(priority=k % 2)
        return carry

    def drain(t, carry):
        for k in range(TOP_K):
            row_copy(0, 0).wait()
        return carry

    lax.fori_loop(0, tt, issue, 0)
    h = h_ref[...]
    x = h.astype(BF16)
    nt = (((1,), (1,)), ((), ()))
    gt = lax.dot_general(x, swg_ref[...], nt, preferred_element_type=F32)
    up = lax.dot_general(x, swu_ref[...], nt, preferred_element_type=F32)
    shared = jnp.dot(((gt * _sigmoid(gt)) * up).astype(BF16), swd_ref[...], preferred_element_type=F32)
    lax.fori_loop(0, tt, drain, 0)
    gf32[...] = gbuf[...].astype(F32)
    gates = gate_ref[...]
    cols = []
    for s in range(SLABS):
        acc = jnp.zeros((tt, LANES), F32)
        for k in range(TOP_K):
            acc = acc + gates[:, k:k + 1] * gf32[pl.ds(k * tt * SLABS + s, tt, stride=SLABS), :]
        cols.append(acc)
    routed = jnp.concatenate(cols, axis=1)
    y = _layer_norm(ALPHA * h + (routed + shared), g_ref[...], b_ref[...])

    @pl.when(i < n_prompt_tiles)
    def _():
        yp_ref[...] = y

    @pl.when(i >= n_prompt_tiles)
    def _():
        ysm_ref[...] = y


def _combine(dest, h_all, gates_tok, swg_t, swu_t, swd, g, b, ys, n_prompt, tt):
    n = h_all.shape[0]
    npt = n_prompt // tt
    vec = pl.BlockSpec((1, D_MODEL), lambda i: (0, 0))
    full = lambda a: pl.BlockSpec(a.shape, lambda i: (0, 0))
    return pl.pallas_call(
        functools.partial(_combine_kernel, n_prompt_tiles=npt), grid=(n // tt,),
        in_specs=[pl.BlockSpec((TOP_K, tt), lambda i: (0, i), memory_space=pltpu.SMEM),
                  pl.BlockSpec((tt, D_MODEL), lambda i: (i, 0)),
                  pl.BlockSpec((tt, TOP_K), lambda i: (i, 0)),
                  full(swg_t), full(swu_t), full(swd), vec, vec,
                  pl.BlockSpec(memory_space=pl.ANY)],
        out_specs=[pl.BlockSpec((tt, D_MODEL), lambda i: (jnp.minimum(i, npt - 1), 0)),
                   pl.BlockSpec((tt, D_MODEL), lambda i: (jnp.maximum(i - npt, 0), 0))],
        out_shape=[jax.ShapeDtypeStruct((n_prompt, D_MODEL), F32),
                   jax.ShapeDtypeStruct((n - n_prompt, D_MODEL), F32)],
        scratch_shapes=[pltpu.VMEM((TOP_K * tt * SLABS, LANES), BF16),
                        pltpu.VMEM((TOP_K * tt * SLABS, LANES), F32), pltpu.SemaphoreType.DMA],
        compiler_params=_cparams(("arbitrary",)), name="combine_shared_ln")(
            dest, h_all, gates_tok, swg_t, swu_t, swd, g, b, ys)


def _mixer(x2d, n_seq, seq_len, s0_bd, shift0, conv0, wts, tm, tn_r, tt_prep, chunk, wkv_nb, tt_conv):
    proj_r = _matmul(x2d, wts["w_r"], tm, tn_r, "inproj_rwkv")
    u = _glu_matmul(x2d, wts["w_ca"], wts["w_cb"], tm, "inproj_glu")
    streams = _rwkv_prep(proj_r, shift0, wts["prep_vecs"], wts["prep_mats"], n_seq, seq_len, tt_prep)
    o_rwkv, s_fin = _wkv(streams, wts["gn_g"], wts["gn_b"], s0_bd, n_seq, seq_len, chunk, wkv_nb)
    conv0_pad = jnp.pad(conv0, ((0, 0), (HALO - (CONV_WIDTH - 1), 0), (0, 0)))
    o_conv = _conv_module(u, conv0_pad, wts["conv_w"], wts["conv_b"], wts["conv_ln_g"], wts["conv_ln_b"],
                          n_seq, seq_len, tt_conv)
    new_shift = proj_r.reshape(n_seq, seq_len, D_SHIFT)[:, seq_len - 1:, :]
    full = jnp.concatenate([conv0, u.reshape(n_seq, seq_len, D_CONV)], axis=1)
    new_conv = full[:, full.shape[1] - (CONV_WIDTH - 1):]
    return o_rwkv, o_conv, _from_block_diag(s_fin), new_shift, new_conv


def kernel(x_prompt, x_sample, state_wkv, state_shift, state_conv, w_in, mu_shift, w0, w2, a0, a2, g2, k_k, k_a, r_k, gn_g, gn_b, conv_w, conv_b, conv_ln_g, conv_ln_b, w_out, ln1_g, ln1_b, w_router, router_bias, exp_w_gate, exp_w_up, exp_w_down, sh_w_gate, sh_w_up, sh_w_down, ln2_g, ln2_b):
    bp, tp, _ = x_prompt.shape
    bs, ts, _ = x_sample.shape
    n_p, n_s = bp * tp, bs * ts
    n_all = n_p + n_s
    row = lambda v: v.reshape(1, -1)
    zpad = jnp.zeros((64, D_RWKV), F32)
    wts = {
        "w_r": w_in[0][:, :D_SHIFT].astype(BF16),
        "w_ca": w_in[0][:, D_SHIFT:D_SHIFT + D_CONV].astype(BF16),
        "w_cb": w_in[0][:, D_SHIFT + D_CONV:].astype(BF16),
        "prep_vecs": [row(mu_shift[0]), row(w0[0]), row(a0[0]), row(k_k[0]), row(k_a[0]), row(r_k[0])],
        "prep_mats": [jnp.concatenate([w2[0], zpad], axis=0).astype(BF16),
                      jnp.concatenate([zpad, a2[0]], axis=0).astype(BF16),
                      g2[0].astype(BF16)],
        "gn_g": row(gn_g[0]), "gn_b": row(gn_b[0]),
        "conv_w": conv_w[0], "conv_b": row(conv_b[0]),
        "conv_ln_g": row(conv_ln_g[0]), "conv_ln_b": row(conv_ln_b[0]),
    }
    w_out_bf = w_out[0].astype(BF16)
    g1, b1 = row(ln1_g[0]), row(ln1_b[0])

    xp = x_prompt.reshape(n_p, D_MODEL)
    zero_s = jnp.zeros((bp, N_PAIRS, LANES, LANES), F32)
    zero_shift = jnp.zeros((bp, 1, D_SHIFT), F32)
    zero_conv = jnp.zeros((bp, CONV_WIDTH - 1, D_CONV), F32)
    orw_p, ocv_p, wkv_p, shift_p, conv_p = _mixer(xp, bp, tp, zero_s, zero_shift, zero_conv, wts,
                                                  tm=512, tn_r=D_SHIFT // 2, tt_prep=256, chunk=64, wkv_nb=bp,
                                                  tt_conv=128)
    xsm = x_sample.reshape(n_s, D_MODEL)
    orw_s, ocv_s, wkv_s, shift_s, conv_s = _mixer(xsm, bs, ts, _to_block_diag(state_wkv[0]), state_shift[0],
                                                  state_conv[0], wts, tm=n_s, tn_r=D_SHIFT // 2, tt_prep=ts,
                                                  chunk=ts, wkv_nb=1, tt_conv=ts)
    h_all, hp_all = _outproj_ln(orw_p, ocv_p, xp, w_out_bf, g1, b1, 256, n_all, 0)
    h_all, hp_all = _outproj_ln(orw_s, ocv_s, xsm, w_out_bf, g1, b1, n_s, n_all, n_p, prev=(h_all, hp_all))

    tt = TOK_TILE
    idx_t, gate_t, rank_t, cnt = _router(h_all, w_router[0].T.astype(BF16), router_bias[0].reshape(-1, 1), tt)
    counts = cnt[:, 0]
    steps = (counts + STEP_ROWS - 1) // STEP_ROWS
    step_end = jnp.cumsum(steps).astype(I32)
    step_start = step_end - steps
    pstart = step_start * STEP_ROWS
    n_steps = _max_steps(n_all * TOP_K)
    n_used = step_end[-1:]
    step = jnp.minimum(jnp.arange(n_steps, dtype=I32), step_end[-1] - 1)
    step_expert = jnp.minimum(jnp.sum((step_end[None, :] <= step[:, None]).astype(I32), axis=1), N_EXPERTS - 1)
    mine = step_expert[:, None] == jnp.arange(N_EXPERTS, dtype=I32)[None, :]
    rows_left = jnp.sum(jnp.where(mine, counts[None, :] - (step[:, None] - step_start[None, :]) * STEP_ROWS, 0),
                        axis=1)
    step_nblk = jnp.clip((rows_left + ROW_BLK - 1) // ROW_BLK, 1, STEP_BLKS).astype(I32)
    dest = _positions(idx_t, rank_t, pstart.reshape(-1, 1), 5 * TOK_TILE)
    xs = _dispatch(dest, hp_all, n_steps * STEP_ROWS, tt)
    ys = _experts(step_expert, step_nblk, n_used, step_end, xs, jnp.swapaxes(exp_w_gate[0], 1, 2),
                  jnp.swapaxes(exp_w_up[0], 1, 2), exp_w_down[0], n_steps)
    y_p, y_s = _combine(dest, h_all, gate_t.T, sh_w_gate[0].T.astype(BF16),
                        sh_w_up[0].T.astype(BF16), sh_w_down[0].astype(BF16), row(ln2_g[0]), row(ln2_b[0]),
                        ys, n_p, tt)
    return (y_p.reshape(bp, tp, D_MODEL), y_s.reshape(bs, ts, D_MODEL),
            wkv_p[None], shift_p[None], conv_p[None], wkv_s[None], shift_s[None], conv_s[None])
```
